```python
import jax, jax.numpy as jnp
from jax import lax
import numpy as np

D_MODEL = 1024
BATCH = 2
SEQ = 8192
DEPTH = 4

N_MIXERS = 3
N_A = (DEPTH + 2) // 3
N_B = (DEPTH + 1) // 3
N_C = DEPTH // 3
D_RNN = 1280
A_HEADS = 16
A_HEAD_DIM = D_RNN // A_HEADS
A_CONV = 4
LRU_C = 8.0
D_SGU = D_MODEL
SGU_CHUNK = 128
SGU_GROUPS = 8
SGU_GROUP_DIM = D_SGU // SGU_GROUPS
D_CONV = D_MODEL
C_CONV = 3
D_FF = 4 * D_MODEL
EPS = 1e-6

kernel_name = "hybrid_rglru_sgu_shortconv_trunk"


def _rmsnorm(x, g):
    x32 = x.astype(jnp.float32)
    y = x32 * lax.rsqrt(jnp.mean(x32 * x32, axis=-1, keepdims=True) + EPS)
    return y.astype(x.dtype) * g


def _causal_dwconv(x, w):
    k = w.shape[0]
    return lax.conv_general_dilated(
        x, w[:, None, :].astype(x.dtype), window_strides=(1,),
        padding=[(k - 1, 0)], dimension_numbers=("NWC", "WIO", "NWC"),
        feature_group_count=x.shape[-1])


def _lru_combine(left, right):
    a_l, b_l = left
    a_r, b_r = right
    return a_l * a_r, a_r * b_l + b_r


def _rglru_mixer(h, w_in, conv_w, conv_b, gate_a_w, gate_a_b, gate_x_w, gate_x_b, lam, w_out):
    b, s, _ = h.shape
    gate, xr = jnp.split(h @ w_in, 2, axis=-1)
    gate = jax.nn.gelu(gate)
    xr = _causal_dwconv(xr, conv_w) + conv_b
    xh = xr.reshape(b, s, A_HEADS, A_HEAD_DIM)
    r = jax.nn.sigmoid(jnp.einsum("bshi,hij->bshj", xh, gate_a_w).reshape(b, s, D_RNN) + gate_a_b)
    ig = jax.nn.sigmoid(jnp.einsum("bshi,hij->bshj", xh, gate_x_w).reshape(b, s, D_RNN) + gate_x_b)
    log_a = -LRU_C * r.astype(jnp.float32) * jax.nn.softplus(-lam.astype(jnp.float32))
    a = jnp.exp(log_a)
    u = jnp.sqrt(-jnp.expm1(2.0 * log_a)) * (ig * xr).astype(jnp.float32)
    _, hs = lax.associative_scan(_lru_combine, (a, u), axis=1)
    return (hs.astype(h.dtype) * gate) @ w_out


def _sgu_mixer(h, w_in, norm_g, w_s, s_bias, w_out):
    b, s, _ = h.shape
    z = jax.nn.gelu(h @ w_in)
    u, v = jnp.split(z, 2, axis=-1)
    v = _rmsnorm(v, norm_g)
    n_chunks = s // SGU_CHUNK
    v = v.reshape(b, n_chunks, SGU_CHUNK, SGU_GROUPS, SGU_GROUP_DIM)
    causal = jnp.tril(jnp.ones((SGU_CHUNK, SGU_CHUNK), dtype=bool))
    w_causal = jnp.where(causal[None], w_s, jnp.zeros((), w_s.dtype))
    mixed = jnp.einsum("gts,bnsgc->bntgc", w_causal, v) + s_bias.T[None, None, :, :, None]
    y = u * mixed.reshape(b, s, D_SGU)
    return y @ w_out


def _shortconv_mixer(h, w_in, conv_w, w_out):
    gb, gc, xv = jnp.split(h @ w_in, 3, axis=-1)
    y = gb * _causal_dwconv(gc * xv, conv_w)
    return y @ w_out


def _sqrelu_mlp(h, w1, w2):
    return jnp.square(jax.nn.relu(h @ w1)) @ w2


def setup_inputs(seed: int = 0) -> dict:
    key = jax.random.key(seed)
    ks = jax.random.split(key, 24)
    f32 = jnp.float32

    def w(k, shape, fan_in):
        return jax.random.normal(k, shape, f32) * (fan_in ** -0.5)

    def gain(k, shape):
        return 1.0 + 0.1 * jax.random.normal(k, shape, f32)

    def bias(k, shape, scale=0.1):
        return scale * jax.random.normal(k, shape, f32)

    a_c = jax.random.uniform(ks[11], (N_A, D_RNN), f32, minval=0.9, maxval=0.999)
    a0 = a_c ** (1.0 / LRU_C)
    a_lambda = jnp.log(a0) - jnp.log1p(-a0)

    return {
        "x": jax.random.normal(ks[0], (BATCH, SEQ, D_MODEL), f32),
        "norm_mix_g": gain(ks[1], (DEPTH, D_MODEL)),
        "norm_mlp_g": gain(ks[2], (DEPTH, D_MODEL)),
        "final_norm_g": gain(ks[3], (D_MODEL,)),
        "a_w_in": w(ks[4], (N_A, D_MODEL, 2 * D_RNN), D_MODEL),
        "a_conv_w": w(ks[5], (N_A, A_CONV, D_RNN), A_CONV),
        "a_conv_b": bias(ks[6], (N_A, D_RNN)),
        "a_gate_a_w": w(ks[7], (N_A, A_HEADS, A_HEAD_DIM, A_HEAD_DIM), A_HEAD_DIM),
        "a_gate_a_b": bias(ks[8], (N_A, D_RNN)),
        "a_gate_x_w": w(ks[9], (N_A, A_HEADS, A_HEAD_DIM, A_HEAD_DIM), A_HEAD_DIM),
        "a_gate_x_b": bias(ks[10], (N_A, D_RNN)),
        "a_lambda": a_lambda,
        "a_w_out": w(ks[12], (N_A, D_RNN, D_MODEL), D_RNN),
        "b_w_in": w(ks[13], (N_B, D_MODEL, 2 * D_SGU), D_MODEL),
        "b_norm_g": gain(ks[14], (N_B, D_SGU)),
        "b_w_s": w(ks[15], (N_B, SGU_GROUPS, SGU_CHUNK, SGU_CHUNK), SGU_CHUNK),
        "b_s_bias": gain(ks[16], (N_B, SGU_GROUPS, SGU_CHUNK)),
        "b_w_out": w(ks[17], (N_B, D_SGU, D_MODEL), D_SGU),
        "c_w_in": w(ks[18], (N_C, D_MODEL, 3 * D_CONV), D_MODEL),
        "c_conv_w": w(ks[19], (N_C, C_CONV, D_CONV), C_CONV),
        "c_w_out": w(ks[20], (N_C, D_CONV, D_MODEL), D_CONV),
        "mlp_w1": w(ks[21], (DEPTH, D_MODEL, D_FF), D_MODEL),
        "mlp_w2": w(ks[22], (DEPTH, D_FF, D_MODEL), D_FF),
    }


def reference(x, norm_mix_g, norm_mlp_g, final_norm_g,
              a_w_in, a_conv_w, a_conv_b, a_gate_a_w, a_gate_a_b, a_gate_x_w, a_gate_x_b, a_lambda, a_w_out,
              b_w_in, b_norm_g, b_w_s, b_s_bias, b_w_out,
              c_w_in, c_conv_w, c_w_out,
              mlp_w1, mlp_w2):
    for i in range(DEPTH):
        kind, j = i % N_MIXERS, i // N_MIXERS
        h = _rmsnorm(x, norm_mix_g[i])
        if kind == 0:
            mix = _rglru_mixer(h, a_w_in[j], a_conv_w[j], a_conv_b[j], a_gate_a_w[j], a_gate_a_b[j],
                               a_gate_x_w[j], a_gate_x_b[j], a_lambda[j], a_w_out[j])
        elif kind == 1:
            mix = _sgu_mixer(h, b_w_in[j], b_norm_g[j], b_w_s[j], b_s_bias[j], b_w_out[j])
        else:
            mix = _shortconv_mixer(h, c_w_in[j], c_conv_w[j], c_w_out[j])
        x = x + mix
        x = x + _sqrelu_mlp(_rmsnorm(x, norm_mlp_g[i]), mlp_w1[i], mlp_w2[i])
    return _rmsnorm(x, final_norm_g)
```

```python
import functools

import jax
import jax.numpy as jnp
from jax import lax
from jax.experimental import pallas as pl
from jax.experimental.pallas import tpu as pltpu

D_MODEL = 1024
DEPTH = 4
N_MIXERS = 3
D_RNN = 1280
A_HEADS = 16
A_HEAD_DIM = D_RNN // A_HEADS
A_CONV = 4
LRU_C = 8.0
D_SGU = D_MODEL
SGU_CHUNK = 128
SGU_GROUPS = 8
SGU_GROUP_DIM = D_SGU // SGU_GROUPS
D_CONV = D_MODEL
C_CONV = 3
D_FF = 4 * D_MODEL
EPS = 1e-6

SUBLANES = 8
HIST = SUBLANES
TS_MIX = 256
TM_MLP = 512
FF_CHUNK = 2048
VMEM_LIMIT = 56 * 1024 * 1024

F32 = jnp.float32
BF16 = jnp.bfloat16


def _rmsnorm(x, g):
    return x * lax.rsqrt(jnp.mean(x * x, axis=-1, keepdims=True) + EPS) * g


def _dot(a, b):
    return jnp.dot(a, b, preferred_element_type=F32)


def _const_spec(shape):
    return pl.BlockSpec(shape, lambda *_: (0,) * len(shape), pipeline_mode=pl.Buffered(1))


def _causal_conv(hist_ref, w_ref, taps, rows):
    acc = None
    for k in range(taps):
        start = HIST - (taps - 1) + k
        term = w_ref[k:k + 1, :] * hist_ref[start:start + rows, :]
        acc = term if acc is None else acc + term
    return acc


def _lru_scan(a_ref, u_ref, h_ref, carry, rows):
    width = a_ref.shape[-1]
    row = lax.broadcasted_iota(jnp.int32, (SUBLANES, width), 0)

    def tile(i, carry):
        r0 = pl.multiple_of(i * SUBLANES, SUBLANES)
        a = a_ref[pl.ds(r0, SUBLANES), :]
        b = u_ref[pl.ds(r0, SUBLANES), :]
        for d in (1, 2, 4):
            keep = row >= d
            a_prev = jnp.where(keep, pltpu.roll(a, d, 0), 1.0)
            b_prev = jnp.where(keep, pltpu.roll(b, d, 0), 0.0)
            b = a * b_prev + b
            a = a * a_prev
        h = a * carry + b
        h_ref[pl.ds(r0, SUBLANES), :] = h
        return jnp.broadcast_to(h[SUBLANES - 1:SUBLANES, :], (SUBLANES, width))

    return lax.fori_loop(0, rows // SUBLANES, tile, carry, unroll=True)


def _rglru_kernel(x_ref, g_ref, w_gate_ref, w_x_ref, conv_w_ref, conv_b_ref,
                  w_ra_ref, b_ra_ref, lam_ref, w_out_ref, o_ref,
                  hist_ref, a_ref, u_ref, h_ref, carry_ref):
    rows = x_ref.shape[1]

    @pl.when(pl.program_id(1) == 0)
    def _():
        hist_ref[0:HIST, :] = jnp.zeros((HIST, D_RNN), F32)
        carry_ref[...] = jnp.zeros((SUBLANES, D_RNN), F32)

    x = x_ref[0]
    h = _rmsnorm(x, g_ref[...]).astype(BF16)
    gate = jax.nn.gelu(_dot(h, w_gate_ref[...]))
    hist_ref[HIST:HIST + rows, :] = _dot(h, w_x_ref[...])
    xr = _causal_conv(hist_ref, conv_w_ref, A_CONV, rows) + conv_b_ref[...]
    hist_ref[0:HIST, :] = hist_ref[rows:rows + HIST, :]

    ra = _dot(xr.astype(BF16), w_ra_ref[...]) + b_ra_ref[...]
    r = jax.nn.sigmoid(ra[:, :D_RNN])
    ig = jax.nn.sigmoid(ra[:, D_RNN:])
    lam = lam_ref[...]
    softplus_neg_lam = jnp.maximum(-lam, 0.0) + jnp.log1p(jnp.exp(-jnp.abs(lam)))
    log_a = -LRU_C * r * softplus_neg_lam
    a = jnp.exp(log_a)
    a_ref[...] = a
    u_ref[...] = jnp.sqrt(1.0 - a * a) * (ig * xr)

    carry_ref[...] = _lru_scan(a_ref, u_ref, h_ref, carry_ref[...], rows)
    y = (h_ref[...] * gate).astype(BF16)
    o_ref[0] = x + _dot(y, w_out_ref[...])


def _rglru_layer(x, g, w_in, conv_w, conv_b, gate_a_w, gate_a_b, gate_x_w, gate_x_b, lam, w_out):
    b, s, _ = x.shape
    ts = TS_MIX
    eye = jnp.eye(A_HEADS, dtype=F32)
    dense_a = jnp.einsum("hij,hk->hikj", gate_a_w, eye).reshape(D_RNN, D_RNN)
    dense_x = jnp.einsum("hij,hk->hikj", gate_x_w, eye).reshape(D_RNN, D_RNN)
    w_ra = jnp.concatenate([dense_a, dense_x], axis=1).astype(BF16)
    b_ra = jnp.concatenate([gate_a_b, gate_x_b])[None, :]
    row_spec = pl.BlockSpec((1, ts, D_MODEL), lambda i, j: (i, j, 0))
    return pl.pallas_call(
        _rglru_kernel,
        grid=(b, s // ts),
        in_specs=[
            row_spec,
            _const_spec((1, D_MODEL)),
            _const_spec((D_MODEL, D_RNN)),
            _const_spec((D_MODEL, D_RNN)),
            _const_spec((A_CONV, D_RNN)),
            _const_spec((1, D_RNN)),
            _const_spec((D_RNN, 2 * D_RNN)),
            _const_spec((1, 2 * D_RNN)),
            _const_spec((1, D_RNN)),
            _const_spec((D_RNN, D_MODEL)),
        ],
        out_specs=row_spec,
        out_shape=jax.ShapeDtypeStruct(x.shape, F32),
        scratch_shapes=[
            pltpu.VMEM((HIST + ts, D_RNN), F32),
            pltpu.VMEM((ts, D_RNN), F32),
            pltpu.VMEM((ts, D_RNN), F32),
            pltpu.VMEM((ts, D_RNN), F32),
            pltpu.VMEM((SUBLANES, D_RNN), F32),
        ],
        compiler_params=pltpu.CompilerParams(
            dimension_semantics=("arbitrary", "arbitrary"), vmem_limit_bytes=VMEM_LIMIT),
        name="rglru_mixer",
    )(x, g[None, :], w_in[:, :D_RNN].astype(BF16), w_in[:, D_RNN:].astype(BF16),
      conv_w, conv_b[None, :], w_ra, b_ra, lam[None, :], w_out.astype(BF16))


def _sgu_kernel(x_ref, g_ref, w_u_ref, w_v_ref, ng_ref, w_s_ref, bias_ref, w_out_ref, o_ref, mix_ref):
    rows = x_ref.shape[1]
    x = x_ref[0]
    h = _rmsnorm(x, g_ref[...]).astype(BF16)
    u = jax.nn.gelu(_dot(h, w_u_ref[...]))
    v = jax.nn.gelu(_dot(h, w_v_ref[...]))
    v = _rmsnorm(v, ng_ref[...]).astype(BF16)

    t_idx = lax.broadcasted_iota(jnp.int32, (SGU_CHUNK, SGU_CHUNK), 0)
    s_idx = lax.broadcasted_iota(jnp.int32, (SGU_CHUNK, SGU_CHUNK), 1)
    causal = t_idx >= s_idx
    for grp in range(SGU_GROUPS):
        w = jnp.where(causal, w_s_ref[grp], 0.0).astype(BF16)
        lanes = slice(grp * SGU_GROUP_DIM, (grp + 1) * SGU_GROUP_DIM)
        for c in range(rows // SGU_CHUNK):
            r0 = c * SGU_CHUNK
            mix_ref[r0:r0 + SGU_CHUNK, lanes] = (
                _dot(w, v[r0:r0 + SGU_CHUNK, lanes]) + bias_ref[:, lanes])
    y = (u * mix_ref[...]).astype(BF16)
    o_ref[0] = x + _dot(y, w_out_ref[...])


def _sgu_layer(x, g, w_in, norm_g, w_s, s_bias, w_out):
    b, s, _ = x.shape
    ts = TS_MIX
    bias = jnp.repeat(s_bias.T, SGU_GROUP_DIM, axis=1)
    row_spec = pl.BlockSpec((1, ts, D_MODEL), lambda i, j: (i, j, 0))
    return pl.pallas_call(
        _sgu_kernel,
        grid=(b, s // ts),
        in_specs=[
            row_spec,
            _const_spec((1, D_MODEL)),
            _const_spec((D_MODEL, D_SGU)),
            _const_spec((D_MODEL, D_SGU)),
            _const_spec((1, D_SGU)),
            _const_spec((SGU_GROUPS, SGU_CHUNK, SGU_CHUNK)),
            _const_spec((SGU_CHUNK, D_SGU)),
            _const_spec((D_SGU, D_MODEL)),
        ],
        out_specs=row_spec,
        out_shape=jax.ShapeDtypeStruct(x.shape, F32),
        scratch_shapes=[pltpu.VMEM((ts, D_SGU), F32)],
        compiler_params=pltpu.CompilerParams(
            dimension_semantics=("arbitrary", "arbitrary"), vmem_limit_bytes=VMEM_LIMIT),
        name="sgu_mixer",
    )(x, g[None, :], w_in[:, :D_SGU].astype(BF16), w_in[:, D_SGU:].astype(BF16),
      norm_g[None, :], w_s, bias, w_out.astype(BF16))


def _shortconv_kernel(x_ref, g_ref, w_b_ref, w_c_ref, w_v_ref, conv_w_ref, w_out_ref, o_ref, hist_ref):
    rows = x_ref.shape[1]

    @pl.when(pl.program_id(1) == 0)
    def _():
        hist_ref[0:HIST, :] = jnp.zeros((HIST, D_CONV), F32)

    x = x_ref[0]
    h = _rmsnorm(x, g_ref[...]).astype(BF16)
    gb = _dot(h, w_b_ref[...])
    hist_ref[HIST:HIST + rows, :] = _dot(h, w_c_ref[...]) * _dot(h, w_v_ref[...])
    y = (gb * _causal_conv(hist_ref, conv_w_ref, C_CONV, rows)).astype(BF16)
    hist_ref[0:HIST, :] = hist_ref[rows:rows + HIST, :]
    o_ref[0] = x + _dot(y, w_out_ref[...])


def _shortconv_layer(x, g, w_in, conv_w, w_out):
    b, s, _ = x.shape
    ts = TS_MIX
    row_spec = pl.BlockSpec((1, ts, D_MODEL), lambda i, j: (i, j, 0))
    w_in = w_in.astype(BF16)
    return pl.pallas_call(
        _shortconv_kernel,
        grid=(b, s // ts),
        in_specs=[
            row_spec,
            _const_spec((1, D_MODEL)),
            _const_spec((D_MODEL, D_CONV)),
            _const_spec((D_MODEL, D_CONV)),
            _const_spec((D_MODEL, D_CONV)),
            _const_spec((C_CONV, D_CONV)),
            _const_spec((D_CONV, D_MODEL)),
        ],
        out_specs=row_spec,
        out_shape=jax.ShapeDtypeStruct(x.shape, F32),
        scratch_shapes=[pltpu.VMEM((HIST + ts, D_CONV), F32)],
        compiler_params=pltpu.CompilerParams(
            dimension_semantics=("arbitrary", "arbitrary"), vmem_limit_bytes=VMEM_LIMIT),
        name="shortconv_mixer",
    )(x, g[None, :], w_in[:, :D_CONV], w_in[:, D_CONV:2 * D_CONV], w_in[:, 2 * D_CONV:],
      conv_w, w_out.astype(BF16))


def _mlp_kernel(x_ref, g_ref, w1_ref, w2_ref, fg_ref, o_ref, *, final_norm):
    x = x_ref[...]
    h = _rmsnorm(x, g_ref[...]).astype(BF16)
    acc = x
    for c in range(D_FF // FF_CHUNK):
        cols = slice(c * FF_CHUNK, (c + 1) * FF_CHUNK)
        a = jnp.square(jnp.maximum(_dot(h, w1_ref[:, cols]), 0.0)).astype(BF16)
        acc = acc + _dot(a, w2_ref[cols, :])
    if final_norm:
        acc = _rmsnorm(acc, fg_ref[...])
    o_ref[...] = acc


def _mlp_layer(x2d, g, w1, w2, final_g, final_norm):
    n, _ = x2d.shape
    tm = TM_MLP
    row_spec = pl.BlockSpec((tm, D_MODEL), lambda i: (i, 0))
    return pl.pallas_call(
        functools.partial(_mlp_kernel, final_norm=final_norm),
        grid=(n // tm,),
        in_specs=[
            row_spec,
            _const_spec((1, D_MODEL)),
            _const_spec((D_MODEL, D_FF)),
            _const_spec((D_FF, D_MODEL)),
            _const_spec((1, D_MODEL)),
        ],
        out_specs=row_spec,
        out_shape=jax.ShapeDtypeStruct(x2d.shape, F32),
        compiler_params=pltpu.CompilerParams(
            dimension_semantics=("arbitrary",), vmem_limit_bytes=VMEM_LIMIT),
        name="sqrelu_mlp",
    )(x2d, g[None, :], w1.astype(BF16), w2.astype(BF16), final_g[None, :])


def kernel(x, norm_mix_g, norm_mlp_g, final_norm_g, a_w_in, a_conv_w, a_conv_b, a_gate_a_w, a_gate_a_b, a_gate_x_w, a_gate_x_b, a_lambda, a_w_out, b_w_in, b_norm_g, b_w_s, b_s_bias, b_w_out, c_w_in, c_conv_w, c_w_out, mlp_w1, mlp_w2):
    b, s, d = x.shape
    assert s % TS_MIX == 0 and (b * s) % TM_MLP == 0 and TS_MIX % SGU_CHUNK == 0
    for i in range(DEPTH):
        kind, j = i % N_MIXERS, i // N_MIXERS
        if kind == 0:
            x = _rglru_layer(x, norm_mix_g[i], a_w_in[j], a_conv_w[j], a_conv_b[j], a_gate_a_w[j],
                             a_gate_a_b[j], a_gate_x_w[j], a_gate_x_b[j], a_lambda[j], a_w_out[j])
        elif kind == 1:
            x = _sgu_layer(x, norm_mix_g[i], b_w_in[j], b_norm_g[j], b_w_s[j], b_s_bias[j], b_w_out[j])
        else:
            x = _shortconv_layer(x, norm_mix_g[i], c_w_in[j], c_conv_w[j], c_w_out[j])
        x = _mlp_layer(x.reshape(b * s, d), norm_mlp_g[i], mlp_w1[i], mlp_w2[i], final_norm_g,
                       final_norm=(i == DEPTH - 1)).reshape(b, s, d)
    return x
```

```python
import functools

import jax
import jax.numpy as jnp
from jax import lax
from jax.experimental import pallas as pl
from jax.experimental.pallas import tpu as pltpu

D_MODEL = 1024
DEPTH = 4
N_MIXERS = 3
D_RNN = 1280
A_HEADS = 16
A_HEAD_DIM = D_RNN // A_HEADS
A_GROUPS = 2
A_GROUP_W = D_RNN // A_GROUPS
A_CONV = 4
LRU_C = 8.0
D_SGU = D_MODEL
SGU_CHUNK = 128
SGU_GROUPS = 8
SGU_GROUP_DIM = D_SGU // SGU_GROUPS
D_CONV = D_MODEL
C_CONV = 3
D_FF = 4 * D_MODEL
EPS = 1e-6
LOG2_E = 1.4426950408889634

SUBLANES = 8
HIST = SUBLANES
TS_MIX = 256
TM_MLP = 512
FF_CHUNK = 2048
VMEM_LIMIT = 56 * 1024 * 1024

F32 = jnp.float32
BF16 = jnp.bfloat16


def _rmsnorm(x, g):
    return x * lax.rsqrt(jnp.mean(x * x, axis=-1, keepdims=True) + EPS) * g


def _dot(a, b):
    return jnp.dot(a, b, preferred_element_type=F32)


def _const_spec(shape):
    return pl.BlockSpec(shape, lambda *_: (0,) * len(shape), pipeline_mode=pl.Buffered(1))


def _causal_conv(hist_ref, w_ref, taps, rows):
    acc = None
    for k in range(taps):
        start = HIST - (taps - 1) + k
        term = w_ref[k:k + 1, :] * hist_ref[start:start + rows, :]
        acc = term if acc is None else acc + term
    return acc


def _sublane_scan(a, b):
    row = lax.broadcasted_iota(jnp.int32, a.shape, 0)
    for d in (1, 2, 4):
        keep = row >= d
        a_prev = jnp.where(keep, pltpu.roll(a, d, 0), 1.0)
        b_prev = jnp.where(keep, pltpu.roll(b, d, 0), 0.0)
        b = a * b_prev + b
        a = a * a_prev
    return a, b


def _segment_permutation(rows, inverse):
    seg = rows // SUBLANES
    i0 = lax.broadcasted_iota(jnp.int32, (rows, rows), 0)
    i1 = lax.broadcasted_iota(jnp.int32, (rows, rows), 1)
    major, natural = (i1, i0) if inverse else (i0, i1)
    hit = natural == (major % SUBLANES) * seg + major // SUBLANES
    return jnp.where(hit, 1.0, 0.0).astype(BF16)


def _rglru_kernel(x_ref, g_ref, w_in_ref, conv_w_ref, conv_b_ref,
                  w_ra_ref, b_ra_ref, lam_ref, w_out_ref, o_ref,
                  ext_ref, tail_ref, a_ref, u_ref, carry_ref):
    rows = x_ref.shape[1]
    seg = rows // SUBLANES
    lead = (A_CONV - 1) * SUBLANES

    @pl.when(pl.program_id(1) == 0)
    def _():
        tail_ref[...] = jnp.zeros(tail_ref.shape, F32)
        carry_ref[...] = jnp.zeros(carry_ref.shape, F32)

    x = x_ref[0]
    h = _rmsnorm(x, g_ref[...]).astype(BF16)
    hp = _dot(_segment_permutation(rows, False), h).astype(BF16)
    row8 = lax.broadcasted_iota(jnp.int32, (SUBLANES, A_GROUP_W), 0)

    ys = []
    for grp in range(A_GROUPS):
        cols = slice(grp * A_GROUP_W, (grp + 1) * A_GROUP_W)
        proj = _dot(hp, w_in_ref[grp])
        ext_ref[lead:lead + rows, cols] = proj[:, :A_GROUP_W]
        for j in range(A_CONV - 1):
            cur = ext_ref[rows + j * SUBLANES:rows + (j + 1) * SUBLANES, cols]
            prev = tail_ref[j * SUBLANES:(j + 1) * SUBLANES, cols]
            ext_ref[j * SUBLANES:(j + 1) * SUBLANES, cols] = jnp.where(
                row8 == 0, pltpu.roll(prev, 1, 0), pltpu.roll(cur, 1, 0))
            tail_ref[j * SUBLANES:(j + 1) * SUBLANES, cols] = cur
        xr = conv_b_ref[:, cols]
        for k in range(A_CONV):
            xr = xr + conv_w_ref[k:k + 1, cols] * ext_ref[k * SUBLANES:k * SUBLANES + rows, cols]

        ra = _dot(xr.astype(BF16), w_ra_ref[grp]) + b_ra_ref[grp]
        r = jax.nn.sigmoid(ra[:, :A_GROUP_W])
        ig = jax.nn.sigmoid(ra[:, A_GROUP_W:])
        lam = lam_ref[:, cols]
        softplus_neg_lam = jnp.maximum(-lam, 0.0) + jnp.log1p(jnp.exp(-jnp.abs(lam)))
        a = jnp.exp2(r * ((-LRU_C * LOG2_E) * softplus_neg_lam))
        a_ref[:, cols] = a
        z = 1.0 - a * a
        u_ref[:, cols] = jnp.where(z > 0.0, z * lax.rsqrt(z), 0.0) * (ig * xr)

        def step(k):
            sl = slice(k * SUBLANES, (k + 1) * SUBLANES)
            return a_ref[sl, cols], u_ref[sl, cols], sl

        prod = jnp.ones((SUBLANES, A_GROUP_W), F32)
        h_loc = jnp.zeros((SUBLANES, A_GROUP_W), F32)
        for k in range(seg):
            a_k, u_k, _ = step(k)
            h_loc = a_k * h_loc + u_k
            prod = a_k * prod
        c_in = carry_ref[:, cols]
        cum_a, cum_b = _sublane_scan(prod, h_loc)
        seg_end = cum_a * c_in + cum_b
        carry_ref[:, cols] = jnp.broadcast_to(seg_end[SUBLANES - 1:SUBLANES, :], seg_end.shape)
        h_cur = jnp.where(row8 == 0, c_in, pltpu.roll(seg_end, 1, 0))
        for k in range(seg):
            a_k, u_k, sl = step(k)
            h_cur = a_k * h_cur + u_k
            u_ref[sl, cols] = h_cur

        gate = jax.nn.gelu(proj[:, A_GROUP_W:])
        ys.append((u_ref[:, cols] * gate).astype(BF16))

    y = _dot(_segment_permutation(rows, True), jnp.concatenate(ys, axis=1)).astype(BF16)
    o_ref[0] = x + _dot(y, w_out_ref[...])


def _rglru_layer(x, g, w_in, conv_w, conv_b, gate_a_w, gate_a_b, gate_x_w, gate_x_b, lam, w_out):
    b, s, _ = x.shape
    ts = TS_MIX
    heads = A_HEADS // A_GROUPS
    lead = (A_CONV - 1) * SUBLANES
    eye = jnp.eye(heads, dtype=F32)

    def dense(w):
        w = w.reshape(A_GROUPS, heads, A_HEAD_DIM, A_HEAD_DIM)
        return jnp.einsum("ghij,hk->ghikj", w, eye).reshape(A_GROUPS, A_GROUP_W, A_GROUP_W)

    w_ra = jnp.concatenate([dense(gate_a_w), dense(gate_x_w)], axis=2).astype(BF16)
    b_ra = jnp.concatenate([gate_a_b.reshape(A_GROUPS, 1, A_GROUP_W),
                            gate_x_b.reshape(A_GROUPS, 1, A_GROUP_W)], axis=2)
    w_gate = w_in[:, :D_RNN].reshape(D_MODEL, A_GROUPS, A_GROUP_W)
    w_x = w_in[:, D_RNN:].reshape(D_MODEL, A_GROUPS, A_GROUP_W)
    w_in_g = jnp.concatenate([w_x, w_gate], axis=2).transpose(1, 0, 2).astype(BF16)
    row_spec = pl.BlockSpec((1, ts, D_MODEL), lambda i, j: (i, j, 0))
    return pl.pallas_call(
        _rglru_kernel,
        grid=(b, s // ts),
        in_specs=[
            row_spec,
            _const_spec((1, D_MODEL)),
            _const_spec((A_GROUPS, D_MODEL, 2 * A_GROUP_W)),
            _const_spec((A_CONV, D_RNN)),
            _const_spec((1, D_RNN)),
            _const_spec((A_GROUPS, A_GROUP_W, 2 * A_GROUP_W)),
            _const_spec((A_GROUPS, 1, 2 * A_GROUP_W)),
            _const_spec((1, D_RNN)),
            _const_spec((D_RNN, D_MODEL)),
        ],
        out_specs=row_spec,
        out_shape=jax.ShapeDtypeStruct(x.shape, F32),
        scratch_shapes=[
            pltpu.VMEM((lead + ts, D_RNN), F32),
            pltpu.VMEM((lead, D_RNN), F32),
            pltpu.VMEM((ts, D_RNN), F32),
            pltpu.VMEM((ts, D_RNN), F32),
            pltpu.VMEM((SUBLANES, D_RNN), F32),
        ],
        compiler_params=pltpu.CompilerParams(
            dimension_semantics=("arbitrary", "arbitrary"), vmem_limit_bytes=VMEM_LIMIT),
        name="rglru_mixer",
    )(x, g[None, :], w_in_g, conv_w, conv_b[None, :], w_ra, b_ra, lam[None, :], w_out.astype(BF16))


def _sgu_kernel(x_ref, g_ref, w_u_ref, w_v_ref, ng_ref, w_s_ref, bias_ref, w_out_ref, o_ref, mix_ref):
    rows = x_ref.shape[1]
    x = x_ref[0]
    h = _rmsnorm(x, g_ref[...]).astype(BF16)
    u = jax.nn.gelu(_dot(h, w_u_ref[...]))
    v = jax.nn.gelu(_dot(h, w_v_ref[...]))
    v = _rmsnorm(v, ng_ref[...]).astype(BF16)

    t_idx = lax.broadcasted_iota(jnp.int32, (SGU_CHUNK, SGU_CHUNK), 0)
    s_idx = lax.broadcasted_iota(jnp.int32, (SGU_CHUNK, SGU_CHUNK), 1)
    causal = t_idx >= s_idx
    for grp in range(SGU_GROUPS):
        w = jnp.where(causal, w_s_ref[grp], 0.0).astype(BF16)
        lanes = slice(grp * SGU_GROUP_DIM, (grp + 1) * SGU_GROUP_DIM)
        for c in range(rows // SGU_CHUNK):
            r0 = c * SGU_CHUNK
            mix_ref[r0:r0 + SGU_CHUNK, lanes] = (
                _dot(w, v[r0:r0 + SGU_CHUNK, lanes]) + bias_ref[:, lanes])
    y = (u * mix_ref[...]).astype(BF16)
    o_ref[0] = x + _dot(y, w_out_ref[...])


def _sgu_layer(x, g, w_in, norm_g, w_s, s_bias, w_out):
    b, s, _ = x.shape
    ts = TS_MIX
    bias = jnp.repeat(s_bias.T, SGU_GROUP_DIM, axis=1)
    row_spec = pl.BlockSpec((1, ts, D_MODEL), lambda i, j: (i, j, 0))
    return pl.pallas_call(
        _sgu_kernel,
        grid=(b, s // ts),
        in_specs=[
            row_spec,
            _const_spec((1, D_MODEL)),
            _const_spec((D_MODEL, D_SGU)),
            _const_spec((D_MODEL, D_SGU)),
            _const_spec((1, D_SGU)),
            _const_spec((SGU_GROUPS, SGU_CHUNK, SGU_CHUNK)),
            _const_spec((SGU_CHUNK, D_SGU)),
            _const_spec((D_SGU, D_MODEL)),
        ],
        out_specs=row_spec,
        out_shape=jax.ShapeDtypeStruct(x.shape, F32),
        scratch_shapes=[pltpu.VMEM((ts, D_SGU), F32)],
        compiler_params=pltpu.CompilerParams(
            dimension_semantics=("arbitrary", "arbitrary"), vmem_limit_bytes=VMEM_LIMIT),
        name="sgu_mixer",
    )(x, g[None, :], w_in[:, :D_SGU].astype(BF16), w_in[:, D_SGU:].astype(BF16),
      norm_g[None, :], w_s, bias, w_out.astype(BF16))


def _shortconv_kernel(x_ref, g_ref, w_b_ref, w_c_ref, w_v_ref, conv_w_ref, w_out_ref, o_ref, hist_ref):
    rows = x_ref.shape[1]

    @pl.when(pl.program_id(1) == 0)
    def _():
        hist_ref[0:HIST, :] = jnp.zeros((HIST, D_CONV), F32)

    x = x_ref[0]
    h = _rmsnorm(x, g_ref[...]).astype(BF16)
    gb = _dot(h, w_b_ref[...])
    hist_ref[HIST:HIST + rows, :] = _dot(h, w_c_ref[...]) * _dot(h, w_v_ref[...])
    y = (gb * _causal_conv(hist_ref, conv_w_ref, C_CONV, rows)).astype(BF16)
    hist_ref[0:HIST, :] = hist_ref[rows:rows + HIST, :]
    o_ref[0] = x + _dot(y, w_out_ref[...])


def _shortconv_layer(x, g, w_in, conv_w, w_out):
    b, s, _ = x.shape
    ts = TS_MIX
    row_spec = pl.BlockSpec((1, ts, D_MODEL), lambda i, j: (i, j, 0))
    w_in = w_in.astype(BF16)
    return pl.pallas_call(
        _shortconv_kernel,
        grid=(b, s // ts),
        in_specs=[
            row_spec,
            _const_spec((1, D_MODEL)),
            _const_spec((D_MODEL, D_CONV)),
            _const_spec((D_MODEL, D_CONV)),
            _const_spec((D_MODEL, D_CONV)),
            _const_spec((C_CONV, D_CONV)),
            _const_spec((D_CONV, D_MODEL)),
        ],
        out_specs=row_spec,
        out_shape=jax.ShapeDtypeStruct(x.shape, F32),
        scratch_shapes=[pltpu.VMEM((HIST + ts, D_CONV), F32)],
        compiler_params=pltpu.CompilerParams(
            dimension_semantics=("arbitrary", "arbitrary"), vmem_limit_bytes=VMEM_LIMIT),
        name="shortconv_mixer",
    )(x, g[None, :], w_in[:, :D_CONV], w_in[:, D_CONV:2 * D_CONV], w_in[:, 2 * D_CONV:],
      conv_w, w_out.astype(BF16))


def _mlp_kernel(x_ref, g_ref, w1_ref, w2_ref, fg_ref, o_ref, *, final_norm):
    x = x_ref[...]
    h = _rmsnorm(x, g_ref[...]).astype(BF16)
    acc = x
    for c in range(D_FF // FF_CHUNK):
        cols = slice(c * FF_CHUNK, (c + 1) * FF_CHUNK)
        a = jnp.square(jnp.maximum(_dot(h, w1_ref[:, cols]), 0.0)).astype(BF16)
        acc = acc + _dot(a, w2_ref[cols, :])
    if final_norm:
        acc = _rmsnorm(acc, fg_ref[...])
    o_ref[...] = acc


def _mlp_layer(x2d, g, w1, w2, final_g, final_norm):
    n, _ = x2d.shape
    tm = TM_MLP
    row_spec = pl.BlockSpec((tm, D_MODEL), lambda i: (i, 0))
    return pl.pallas_call(
        functools.partial(_mlp_kernel, final_norm=final_norm),
        grid=(n // tm,),
        in_specs=[
            row_spec,
            _const_spec((1, D_MODEL)),
            _const_spec((D_MODEL, D_FF)),
            _const_spec((D_FF, D_MODEL)),
            _const_spec((1, D_MODEL)),
        ],
        out_specs=row_spec,
        out_shape=jax.ShapeDtypeStruct(x2d.shape, F32),
        compiler_params=pltpu.CompilerParams(
            dimension_semantics=("arbitrary",), vmem_limit_bytes=VMEM_LIMIT),
        name="sqrelu_mlp",
    )(x2d, g[None, :], w1.astype(BF16), w2.astype(BF16), final_g[None, :])


def kernel(x, norm_mix_g, norm_mlp_g, final_norm_g, a_w_in, a_conv_w, a_conv_b, a_gate_a_w, a_gate_a_b, a_gate_x_w, a_gate_x_b, a_lambda, a_w_out, b_w_in, b_norm_g, b_w_s, b_s_bias, b_w_out, c_w_in, c_conv_w, c_w_out, mlp_w1, mlp_w2):
    b, s, d = x.shape
    assert s % TS_MIX == 0 and (b * s) % TM_MLP == 0 and TS_MIX % SGU_CHUNK == 0
    for i in range(DEPTH):
        kind, j = i % N_MIXERS, i // N_MIXERS
        if kind == 0:
            x = _rglru_layer(x, norm_mix_g[i], a_w_in[j], a_conv_w[j], a_conv_b[j], a_gate_a_w[j],
                             a_gate_a_b[j], a_gate_x_w[j], a_gate_x_b[j], a_lambda[j], a_w_out[j])
        elif kind == 1:
            x = _sgu_layer(x, norm_mix_g[i], b_w_in[j], b_norm_g[j], b_w_s[j], b_s_bias[j], b_w_out[j])
        else:
            x = _shortconv_layer(x, norm_mix_g[i], c_w_in[j], c_conv_w[j], c_w_out[j])
        x = _mlp_layer(x.reshape(b * s, d), norm_mlp_g[i], mlp_w1[i], mlp_w2[i], final_norm_g,
                       final_norm=(i == DEPTH - 1)).reshape(b, s, d)
    return x
```

```python
import functools

import jax
import jax.numpy as jnp
import numpy as np
from jax import lax
from jax.experimental import pallas as pl
from jax.experimental.pallas import tpu as pltpu

D_MODEL = 1024
DEPTH = 4
N_MIXERS = 3
D_RNN = 1280
A_HEADS = 16
A_HEAD_DIM = D_RNN // A_HEADS
A_CONV = 4
LRU_C = 8.0
D_SGU = D_MODEL
SGU_CHUNK = 128
SGU_GROUPS = 8
SGU_GROUP_DIM = D_SGU // SGU_GROUPS
D_CONV = D_MODEL
C_CONV = 3
D_FF = 4 * D_MODEL
EPS = 1e-6
LOG2_E = 1.4426950408889634

SUBLANES = 8
BF16_ROWS = 16
LANES = 128
MXU_COLS = 256
COL_TILE = 2 * MXU_COLS
A_BAND = MXU_COLS
A_BANDS = D_RNN // A_BAND
A_WINDOW = 2 * A_BAND
HIST = SUBLANES
TS_MIX = 512
SUB_ROWS = 256
TM_MLP = 512
FF_CHUNK = 2048
VMEM_LIMIT = 56 * 1024 * 1024

F32 = jnp.float32
BF16 = jnp.bfloat16


def _rmsnorm(x, g):
    return x * lax.rsqrt(jnp.mean(x * x, axis=-1, keepdims=True) + EPS) * g


def _dot(a, b):
    return jnp.dot(a, b, preferred_element_type=F32)


def _const_spec(shape):
    return pl.BlockSpec(shape, lambda *_: (0,) * len(shape), pipeline_mode=pl.Buffered(1))


def _is_first_step():
    return jnp.logical_and(pl.program_id(0) == 0, pl.program_id(1) == 0)


def _mixer_call(body, name, x, consts, scratch_shapes, mlp_w1, mlp_w2):
    b, s, _ = x.shape
    nt = s // TS_MIX
    steps = b * nt
    row_spec = pl.BlockSpec((1, TS_MIX, D_MODEL), lambda i, j: (i, j, 0))

    def slice_spec(w):
        rows = w.shape[0] // steps
        assert rows * steps == w.shape[0] and rows % BF16_ROWS == 0
        return pl.BlockSpec((rows, w.shape[1]), lambda i, j: (i * nt + j, 0))

    n_const = len(consts)

    def kernel(x_ref, *refs):
        const_refs = refs[:n_const]
        w1_ref, w2_ref, o_ref, w1_out_ref, w2_out_ref = refs[n_const:n_const + 5]
        w1_out_ref[...] = w1_ref[...].astype(BF16)
        w2_out_ref[...] = w2_ref[...].astype(BF16)
        body(x_ref, *const_refs, o_ref, *refs[n_const + 5:])

    return pl.pallas_call(
        kernel,
        grid=(b, nt),
        in_specs=[row_spec] + [_const_spec(c.shape) for c in consts]
        + [slice_spec(mlp_w1), slice_spec(mlp_w2)],
        out_specs=[row_spec, slice_spec(mlp_w1), slice_spec(mlp_w2)],
        out_shape=[jax.ShapeDtypeStruct(x.shape, F32),
                   jax.ShapeDtypeStruct(mlp_w1.shape, BF16),
                   jax.ShapeDtypeStruct(mlp_w2.shape, BF16)],
        scratch_shapes=scratch_shapes,
        compiler_params=pltpu.CompilerParams(
            dimension_semantics=("arbitrary", "arbitrary"), vmem_limit_bytes=VMEM_LIMIT),
        name=name,
    )(x, *consts, mlp_w1, mlp_w2)


def _sublane_scan(a, b):
    row = lax.broadcasted_iota(jnp.int32, a.shape, 0)
    for d in (1, 2, 4):
        keep = row >= d
        a_prev = jnp.where(keep, pltpu.roll(a, d, 0), 1.0)
        b_prev = jnp.where(keep, pltpu.roll(b, d, 0), 0.0)
        b = a * b_prev + b
        a = a * a_prev
    return a, b


def _segment_permutation(rows, inverse):
    seg = rows // SUBLANES
    i0 = lax.broadcasted_iota(jnp.int32, (rows, rows), 0)
    i1 = lax.broadcasted_iota(jnp.int32, (rows, rows), 1)
    major, natural = (i1, i0) if inverse else (i0, i1)
    hit = natural == (major % SUBLANES) * seg + major // SUBLANES
    return jnp.where(hit, 1.0, 0.0).astype(BF16)


def _band_window_starts():
    starts = []
    for band in range(A_BANDS):
        first_head = (band * A_BAND) // A_HEAD_DIM
        last_head = -(-((band + 1) * A_BAND) // A_HEAD_DIM)
        lo = min((first_head * A_HEAD_DIM) // LANES * LANES, D_RNN - A_WINDOW)
        assert lo <= first_head * A_HEAD_DIM and last_head * A_HEAD_DIM <= lo + A_WINDOW
        starts.append(lo)
    return starts


def _rglru_body(x_ref, g_ref, w_in_ref, conv_w_ref, conv_b_ref, gate_a_ref, gate_x_ref,
                head_tile_ref, head_mask_ref, b_a_ref, b_x_ref, lam_ref, w_out_ref, o_ref,
                w_in_s, w_band_s, w_out_s, ext_ref, tail_ref, xr_ref, a_ref, u_ref, carry_ref):
    window_starts = _band_window_starts()

    @pl.when(_is_first_step())
    def _():
        w_in_s[:, :D_RNN] = w_in_ref[:, D_RNN:].astype(BF16)
        w_in_s[:, D_RNN:] = w_in_ref[:, :D_RNN].astype(BF16)
        w_out_s[...] = w_out_ref[...].astype(BF16)
        for band, lo in enumerate(window_starts):
            for part, gate_ref in enumerate((gate_a_ref, gate_x_ref)):
                tiled = _dot(gate_ref[lo:lo + A_WINDOW, :].astype(BF16), head_tile_ref[band])
                w_band_s[band, :, part * A_BAND:(part + 1) * A_BAND] = (
                    tiled * head_mask_ref[band]).astype(BF16)

    @pl.when(pl.program_id(1) == 0)
    def _():
        tail_ref[...] = jnp.zeros(tail_ref.shape, F32)
        carry_ref[...] = jnp.zeros(carry_ref.shape, F32)

    for sub in range(x_ref.shape[1] // SUB_ROWS):
        r0 = sub * SUB_ROWS
        o_ref[0, r0:r0 + SUB_ROWS, :] = _rglru_sub_block(
            x_ref[0, r0:r0 + SUB_ROWS, :], g_ref, conv_w_ref, conv_b_ref, b_a_ref, b_x_ref, lam_ref,
            w_in_s, w_band_s, w_out_s, ext_ref.at[sub], tail_ref, xr_ref.at[sub], a_ref.at[sub],
            u_ref.at[sub], carry_ref)


def _rglru_sub_block(x, g_ref, conv_w_ref, conv_b_ref, b_a_ref, b_x_ref, lam_ref,
                     w_in_s, w_band_s, w_out_s, ext_ref, tail_ref, xr_ref, a_ref, u_ref, carry_ref):
    rows = SUB_ROWS
    seg = rows // SUBLANES
    lead = (A_CONV - 1) * SUBLANES
    window_starts = _band_window_starts()
    h = _rmsnorm(x, g_ref[...]).astype(BF16)
    hp = _dot(_segment_permutation(rows, False), h).astype(BF16)
    proj = _dot(hp, w_in_s[...])
    ext_ref[lead:lead + rows, :] = proj[:, :D_RNN]
    row8 = lax.broadcasted_iota(jnp.int32, (SUBLANES, D_RNN), 0)
    for j in range(A_CONV - 1):
        sl = slice(j * SUBLANES, (j + 1) * SUBLANES)
        cur = ext_ref[rows + j * SUBLANES:rows + (j + 1) * SUBLANES, :]
        ext_ref[sl, :] = jnp.where(row8 == 0, pltpu.roll(tail_ref[sl, :], 1, 0), pltpu.roll(cur, 1, 0))
        tail_ref[sl, :] = cur
    xr_all = conv_b_ref[...]
    for k in range(A_CONV):
        xr_all = xr_all + conv_w_ref[k:k + 1, :] * ext_ref[k * SUBLANES:k * SUBLANES + rows, :]
    xr_ref[...] = xr_all
    xr_bf16 = xr_all.astype(BF16)
    row8 = lax.broadcasted_iota(jnp.int32, (SUBLANES, A_BAND), 0)

    ys = []
    for band, lo in enumerate(window_starts):
        cols = slice(band * A_BAND, (band + 1) * A_BAND)
        xr = xr_ref[:, cols]
        ra = _dot(xr_bf16[:, lo:lo + A_WINDOW], w_band_s[band])
        r = jax.nn.sigmoid(ra[:, :A_BAND] + b_a_ref[:, cols])
        ig = jax.nn.sigmoid(ra[:, A_BAND:] + b_x_ref[:, cols])
        lam = lam_ref[:, cols]
        softplus_neg_lam = jnp.maximum(-lam, 0.0) + jnp.log1p(jnp.exp(-jnp.abs(lam)))
        a = jnp.exp2(r * ((-LRU_C * LOG2_E) * softplus_neg_lam))
        a_ref[:, cols] = a
        z = 1.0 - a * a
        u_ref[:, cols] = jnp.where(z > 0.0, z * lax.rsqrt(z), 0.0) * (ig * xr)

        def step(k):
            sl = slice(k * SUBLANES, (k + 1) * SUBLANES)
            return a_ref[sl, cols], u_ref[sl, cols], sl

        prod = jnp.ones((SUBLANES, A_BAND), F32)
        h_loc = jnp.zeros((SUBLANES, A_BAND), F32)
        for k in range(seg):
            a_k, u_k, _ = step(k)
            h_loc = a_k * h_loc + u_k
            prod = a_k * prod
        c_in = carry_ref[:, cols]
        cum_a, cum_b = _sublane_scan(prod, h_loc)
        seg_end = cum_a * c_in + cum_b
        carry_ref[:, cols] = jnp.broadcast_to(seg_end[SUBLANES - 1:SUBLANES, :], seg_end.shape)
        h_cur = jnp.where(row8 == 0, c_in, pltpu.roll(seg_end, 1, 0))
        for k in range(seg):
            a_k, u_k, sl = step(k)
            h_cur = a_k * h_cur + u_k
            u_ref[sl, cols] = h_cur

        gate = jax.nn.gelu(proj[:, D_RNN + band * A_BAND:D_RNN + (band + 1) * A_BAND])
        ys.append((u_ref[:, cols] * gate).astype(BF16))

    y = _dot(_segment_permutation(rows, True), jnp.concatenate(ys, axis=1)).astype(BF16)
    return x + _dot(y, w_out_s[...])


def _rglru_layer(x, g, w_in, conv_w, conv_b, gate_a_w, gate_a_b, gate_x_w, gate_x_b, lam, w_out,
                 mlp_w1, mlp_w2):
    lead = (A_CONV - 1) * SUBLANES
    n_sub = TS_MIX // SUB_ROWS
    head_tile = np.zeros((A_BANDS, A_HEAD_DIM, A_BAND), np.float32)
    head_mask = np.zeros((A_BANDS, A_WINDOW, A_BAND), np.float32)
    for band, lo in enumerate(_band_window_starts()):
        col = band * A_BAND + np.arange(A_BAND)
        row = lo + np.arange(A_WINDOW)
        head_tile[band] = col[None, :] % A_HEAD_DIM == np.arange(A_HEAD_DIM)[:, None]
        head_mask[band] = row[:, None] // A_HEAD_DIM == col[None, :] // A_HEAD_DIM
    consts = [
        g[None, :], w_in, conv_w, conv_b[None, :],
        gate_a_w.reshape(D_RNN, A_HEAD_DIM), gate_x_w.reshape(D_RNN, A_HEAD_DIM),
        jnp.asarray(head_tile, BF16), jnp.asarray(head_mask, F32),
        gate_a_b[None, :], gate_x_b[None, :], lam[None, :], w_out,
    ]
    scratch = [
        pltpu.VMEM((D_MODEL, 2 * D_RNN), BF16),
        pltpu.VMEM((A_BANDS, A_WINDOW, 2 * A_BAND), BF16),
        pltpu.VMEM((D_RNN, D_MODEL), BF16),
        pltpu.VMEM((n_sub, lead + SUB_ROWS, D_RNN), F32),
        pltpu.VMEM((lead, D_RNN), F32),
        pltpu.VMEM((n_sub, SUB_ROWS, D_RNN), F32),
        pltpu.VMEM((n_sub, SUB_ROWS, D_RNN), F32),
        pltpu.VMEM((n_sub, SUB_ROWS, D_RNN), F32),
        pltpu.VMEM((SUBLANES, D_RNN), F32),
    ]
    return _mixer_call(_rglru_body, "rglru_mixer", x, consts, scratch, mlp_w1, mlp_w2)


def _sgu_body(x_ref, g_ref, w_in_ref, ng_ref, w_s_ref, bias_ref, w_out_ref, o_ref,
              w_u_s, w_v_s, w_c_s, w_out_s, mix_ref):
    rows = x_ref.shape[1]

    @pl.when(_is_first_step())
    def _():
        w_u_s[...] = w_in_ref[:, :D_SGU].astype(BF16)
        w_v_s[...] = w_in_ref[:, D_SGU:].astype(BF16)
        w_out_s[...] = w_out_ref[...].astype(BF16)
        t_idx = lax.broadcasted_iota(jnp.int32, (SGU_CHUNK, SGU_CHUNK), 0)
        s_idx = lax.broadcasted_iota(jnp.int32, (SGU_CHUNK, SGU_CHUNK), 1)
        for grp in range(SGU_GROUPS):
            w_c_s[grp] = jnp.where(t_idx >= s_idx, w_s_ref[grp], 0.0).astype(BF16)

    for r0 in range(0, rows, SUB_ROWS):
        x = x_ref[0, r0:r0 + SUB_ROWS, :]
        h = _rmsnorm(x, g_ref[...]).astype(BF16)
        v = jax.nn.gelu(_dot(h, w_v_s[...]))
        v = _rmsnorm(v, ng_ref[...]).astype(BF16)
        u = jax.nn.gelu(_dot(h, w_u_s[...]))
        for grp in range(SGU_GROUPS):
            lanes = slice(grp * SGU_GROUP_DIM, (grp + 1) * SGU_GROUP_DIM)
            for c0 in range(0, SUB_ROWS, SGU_CHUNK):
                mix_ref[r0 + c0:r0 + c0 + SGU_CHUNK, lanes] = (
                    _dot(w_c_s[grp], v[c0:c0 + SGU_CHUNK, lanes]) + bias_ref[:, lanes])
        y = (u * mix_ref[r0:r0 + SUB_ROWS, :]).astype(BF16)
        o_ref[0, r0:r0 + SUB_ROWS, :] = x + _dot(y, w_out_s[...])


def _sgu_layer(x, g, w_in, norm_g, w_s, s_bias, w_out, mlp_w1, mlp_w2):
    bias = jnp.repeat(s_bias.T, SGU_GROUP_DIM, axis=1)
    consts = [g[None, :], w_in, norm_g[None, :], w_s, bias, w_out]
    scratch = [
        pltpu.VMEM((D_MODEL, D_SGU), BF16),
        pltpu.VMEM((D_MODEL, D_SGU), BF16),
        pltpu.VMEM((SGU_GROUPS, SGU_CHUNK, SGU_CHUNK), BF16),
        pltpu.VMEM((D_SGU, D_MODEL), BF16),
        pltpu.VMEM((TS_MIX, D_SGU), F32),
    ]
    return _mixer_call(_sgu_body, "sgu_mixer", x, consts, scratch, mlp_w1, mlp_w2)


def _shortconv_body(x_ref, g_ref, w_in_ref, conv_w_ref, w_out_ref, o_ref, w_in_s, w_out_s, hist_ref):
    rows = x_ref.shape[1]

    @pl.when(_is_first_step())
    def _():
        w_in_s[...] = w_in_ref[...].astype(BF16)
        w_out_s[...] = w_out_ref[...].astype(BF16)

    @pl.when(pl.program_id(1) == 0)
    def _():
        hist_ref[0:HIST, :] = jnp.zeros((HIST, D_CONV), F32)

    for r0 in range(0, rows, SUB_ROWS):
        x = x_ref[0, r0:r0 + SUB_ROWS, :]
        h = _rmsnorm(x, g_ref[...]).astype(BF16)
        gb = _dot(h, w_in_s[:, :D_CONV])
        gc = _dot(h, w_in_s[:, D_CONV:2 * D_CONV])
        xv = _dot(h, w_in_s[:, 2 * D_CONV:])
        hist_ref[HIST + r0:HIST + r0 + SUB_ROWS, :] = gc * xv
        conv = None
        for k in range(C_CONV):
            start = HIST + r0 - (C_CONV - 1) + k
            term = conv_w_ref[k:k + 1, :] * hist_ref[start:start + SUB_ROWS, :]
            conv = term if conv is None else conv + term
        y = (gb * conv).astype(BF16)
        o_ref[0, r0:r0 + SUB_ROWS, :] = x + _dot(y, w_out_s[...])
    hist_ref[0:HIST, :] = hist_ref[rows:rows + HIST, :]


def _shortconv_layer(x, g, w_in, conv_w, w_out, mlp_w1, mlp_w2):
    consts = [g[None, :], w_in, conv_w, w_out]
    scratch = [
        pltpu.VMEM((D_MODEL, 3 * D_CONV), BF16),
        pltpu.VMEM((D_CONV, D_MODEL), BF16),
        pltpu.VMEM((HIST + TS_MIX, D_CONV), F32),
    ]
    return _mixer_call(_shortconv_body, "shortconv_mixer", x, consts, scratch, mlp_w1, mlp_w2)


def _mlp_kernel(x_ref, g_ref, w1_ref, w2_ref, fg_ref, o_ref, *, final_norm):
    x = x_ref[...]
    h = _rmsnorm(x, g_ref[...]).astype(BF16)
    acc = x
    for c in range(D_FF // FF_CHUNK):
        cols = slice(c * FF_CHUNK, (c + 1) * FF_CHUNK)
        a = jnp.square(jnp.maximum(_dot(h, w1_ref[:, cols]), 0.0)).astype(BF16)
        acc = acc + _dot(a, w2_ref[cols, :])
    if final_norm:
        acc = _rmsnorm(acc, fg_ref[...])
    o_ref[...] = acc


def _mlp_layer(x2d, g, w1_bf16, w2_bf16, final_g, final_norm):
    n, _ = x2d.shape
    tm = TM_MLP
    row_spec = pl.BlockSpec((tm, D_MODEL), lambda i: (i, 0))
    return pl.pallas_call(
        functools.partial(_mlp_kernel, final_norm=final_norm),
        grid=(n // tm,),
        in_specs=[
            row_spec,
            _const_spec((1, D_MODEL)),
            _const_spec((D_MODEL, D_FF)),
            _const_spec((D_FF, D_MODEL)),
            _const_spec((1, D_MODEL)),
        ],
        out_specs=row_spec,
        out_shape=jax.ShapeDtypeStruct(x2d.shape, F32),
        compiler_params=pltpu.CompilerParams(
            dimension_semantics=("arbitrary",), vmem_limit_bytes=VMEM_LIMIT),
        name="sqrelu_mlp",
    )(x2d, g[None, :], w1_bf16, w2_bf16, final_g[None, :])


def kernel(x, norm_mix_g, norm_mlp_g, final_norm_g, a_w_in, a_conv_w, a_conv_b, a_gate_a_w, a_gate_a_b, a_gate_x_w, a_gate_x_b, a_lambda, a_w_out, b_w_in, b_norm_g, b_w_s, b_s_bias, b_w_out, c_w_in, c_conv_w, c_w_out, mlp_w1, mlp_w2):
    b, s, d = x.shape
    assert s % TS_MIX == 0 and (b * s) % TM_MLP == 0
    assert TS_MIX % SUB_ROWS == 0 and SUB_ROWS % SGU_CHUNK == 0
    for i in range(DEPTH):
        kind, j = i % N_MIXERS, i // N_MIXERS
        w1, w2 = mlp_w1[i], mlp_w2[i]
        if kind == 0:
            x, w1, w2 = _rglru_layer(x, norm_mix_g[i], a_w_in[j], a_conv_w[j], a_conv_b[j], a_gate_a_w[j],
                                     a_gate_a_b[j], a_gate_x_w[j], a_gate_x_b[j], a_lambda[j], a_w_out[j],
                                     w1, w2)
        elif kind == 1:
            x, w1, w2 = _sgu_layer(x, norm_mix_g[i], b_w_in[j], b_norm_g[j], b_w_s[j], b_s_bias[j],
                                   b_w_out[j], w1, w2)
        else:
            x, w1, w2 = _shortconv_layer(x, norm_mix_g[i], c_w_in[j], c_conv_w[j], c_w_out[j], w1, w2)
        x = _mlp_layer(x.reshape(b * s, d), norm_mlp_g[i], w1, w2, final_norm_g,
                       final_norm=(i == DEPTH - 1)).reshape(b, s, d)
    return x
```

```python
import functools

import jax
import jax.numpy as jnp
import numpy as np
from jax import lax
from jax.experimental import pallas as pl
from jax.experimental.pallas import tpu as pltpu

D_MODEL = 1024
DEPTH = 4
N_MIXERS = 3
D_RNN = 1280
A_HEADS = 16
A_HEAD_DIM = D_RNN // A_HEADS
A_CONV = 4
LRU_C = 8.0
D_SGU = D_MODEL
SGU_CHUNK = 128
SGU_GROUPS = 8
SGU_GROUP_DIM = D_SGU // SGU_GROUPS
D_CONV = D_MODEL
C_CONV = 3
D_FF = 4 * D_MODEL
EPS = 1e-6
LOG2_E = 1.4426950408889634

SUBLANES = 8
BF16_ROWS = 16
LANES = 128
MXU_COLS = 256
COL_TILE = 2 * MXU_COLS
A_BAND = MXU_COLS
A_BANDS = D_RNN // A_BAND
A_WINDOW = 2 * A_BAND
HIST = SUBLANES
TS_MIX = 512
SUB_ROWS = 256
TM_MLP = 512
FF_CHUNK = 2048
VMEM_LIMIT = 56 * 1024 * 1024

F32 = jnp.float32
BF16 = jnp.bfloat16


def _rmsnorm(x, g):
    return x * lax.rsqrt(jnp.mean(x * x, axis=-1, keepdims=True) + EPS) * g


def _dot(a, b):
    return jnp.dot(a, b, preferred_element_type=F32)


def _layer_spec(stacked, layer):
    zeros = (0,) * (stacked.ndim - 1)
    return pl.BlockSpec((None,) + stacked.shape[1:], lambda *_: (layer,) + zeros,
                        pipeline_mode=pl.Buffered(1))


def _row(stacked):
    return stacked[:, None, :]


def _is_first_step():
    return jnp.logical_and(pl.program_id(0) == 0, pl.program_id(1) == 0)


def _mixer_call(body, name, x, consts, scratch_shapes, mlp_w1, mlp_w2, mlp_layer):
    b, s, _ = x.shape
    nt = s // TS_MIX
    steps = b * nt
    row_spec = pl.BlockSpec((1, TS_MIX, D_MODEL), lambda i, j: (i, j, 0))

    def slice_rows(w):
        rows = w.shape[1] // steps
        assert rows * steps == w.shape[1] and rows % BF16_ROWS == 0
        return rows

    def in_slice_spec(w):
        return pl.BlockSpec((None, slice_rows(w), w.shape[2]), lambda i, j: (mlp_layer, i * nt + j, 0))

    def out_slice_spec(w):
        return pl.BlockSpec((slice_rows(w), w.shape[2]), lambda i, j: (i * nt + j, 0))

    n_const = len(consts)

    def kernel(x_ref, *refs):
        const_refs = refs[:n_const]
        w1_ref, w2_ref, o_ref, w1_out_ref, w2_out_ref = refs[n_const:n_const + 5]
        w1_out_ref[...] = w1_ref[...].astype(BF16)
        w2_out_ref[...] = w2_ref[...].astype(BF16)
        body(x_ref, *const_refs, o_ref, *refs[n_const + 5:])

    return pl.pallas_call(
        kernel,
        grid=(b, nt),
        in_specs=[row_spec] + [_layer_spec(arr, layer) for arr, layer in consts]
        + [in_slice_spec(mlp_w1), in_slice_spec(mlp_w2)],
        out_specs=[row_spec, out_slice_spec(mlp_w1), out_slice_spec(mlp_w2)],
        out_shape=[jax.ShapeDtypeStruct(x.shape, F32),
                   jax.ShapeDtypeStruct(mlp_w1.shape[1:], BF16),
                   jax.ShapeDtypeStruct(mlp_w2.shape[1:], BF16)],
        scratch_shapes=scratch_shapes,
        compiler_params=pltpu.CompilerParams(
            dimension_semantics=("arbitrary", "arbitrary"), vmem_limit_bytes=VMEM_LIMIT),
        name=name,
    )(x, *[arr for arr, _ in consts], mlp_w1, mlp_w2)


def _sublane_scan(a, b):
    row = lax.broadcasted_iota(jnp.int32, a.shape, 0)
    for d in (1, 2, 4):
        keep = row >= d
        a_prev = jnp.where(keep, pltpu.roll(a, d, 0), 1.0)
        b_prev = jnp.where(keep, pltpu.roll(b, d, 0), 0.0)
        b = a * b_prev + b
        a = a * a_prev
    return a, b


def _segment_permutation(rows, inverse):
    seg = rows // SUBLANES
    i0 = lax.broadcasted_iota(jnp.int32, (rows, rows), 0)
    i1 = lax.broadcasted_iota(jnp.int32, (rows, rows), 1)
    major, natural = (i1, i0) if inverse else (i0, i1)
    hit = natural == (major % SUBLANES) * seg + major // SUBLANES
    return jnp.where(hit, 1.0, 0.0).astype(BF16)


def _band_window_starts():
    starts = []
    for band in range(A_BANDS):
        first_head = (band * A_BAND) // A_HEAD_DIM
        last_head = -(-((band + 1) * A_BAND) // A_HEAD_DIM)
        lo = min((first_head * A_HEAD_DIM) // LANES * LANES, D_RNN - A_WINDOW)
        assert lo <= first_head * A_HEAD_DIM and last_head * A_HEAD_DIM <= lo + A_WINDOW
        starts.append(lo)
    return starts


def _rglru_body(x_ref, g_ref, w_in_ref, conv_w_ref, conv_b_ref, gate_a_ref, gate_x_ref,
                head_tile_ref, head_mask_ref, b_a_ref, b_x_ref, lam_ref, w_out_ref, o_ref,
                w_in_s, w_band_s, w_out_s, ext_ref, tail_ref, xr_ref, a_ref, u_ref, carry_ref):
    window_starts = _band_window_starts()

    @pl.when(_is_first_step())
    def _():
        w_in_s[:, :D_RNN] = w_in_ref[:, D_RNN:].astype(BF16)
        w_in_s[:, D_RNN:] = w_in_ref[:, :D_RNN].astype(BF16)
        w_out_s[...] = w_out_ref[...].astype(BF16)
        for band, lo in enumerate(window_starts):
            for part, gate_ref in enumerate((gate_a_ref, gate_x_ref)):
                tiled = _dot(gate_ref[lo:lo + A_WINDOW, :].astype(BF16), head_tile_ref[band])
                w_band_s[band, :, part * A_BAND:(part + 1) * A_BAND] = (
                    tiled * head_mask_ref[band]).astype(BF16)

    @pl.when(pl.program_id(1) == 0)
    def _():
        tail_ref[...] = jnp.zeros(tail_ref.shape, F32)
        carry_ref[...] = jnp.zeros(carry_ref.shape, F32)

    for sub in range(x_ref.shape[1] // SUB_ROWS):
        r0 = sub * SUB_ROWS
        o_ref[0, r0:r0 + SUB_ROWS, :] = _rglru_sub_block(
            x_ref[0, r0:r0 + SUB_ROWS, :], g_ref, conv_w_ref, conv_b_ref, b_a_ref, b_x_ref, lam_ref,
            w_in_s, w_band_s, w_out_s, ext_ref.at[sub], tail_ref, xr_ref.at[sub], a_ref.at[sub],
            u_ref.at[sub], carry_ref)


def _rglru_sub_block(x, g_ref, conv_w_ref, conv_b_ref, b_a_ref, b_x_ref, lam_ref,
                     w_in_s, w_band_s, w_out_s, ext_ref, tail_ref, xr_ref, a_ref, u_ref, carry_ref):
    rows = SUB_ROWS
    seg = rows // SUBLANES
    lead = (A_CONV - 1) * SUBLANES
    window_starts = _band_window_starts()
    h = _rmsnorm(x, g_ref[...]).astype(BF16)
    hp = _dot(_segment_permutation(rows, False), h).astype(BF16)
    proj = _dot(hp, w_in_s[...])
    ext_ref[lead:lead + rows, :] = proj[:, :D_RNN]
    row8 = lax.broadcasted_iota(jnp.int32, (SUBLANES, D_RNN), 0)
    for j in range(A_CONV - 1):
        sl = slice(j * SUBLANES, (j + 1) * SUBLANES)
        cur = ext_ref[rows + j * SUBLANES:rows + (j + 1) * SUBLANES, :]
        ext_ref[sl, :] = jnp.where(row8 == 0, pltpu.roll(tail_ref[sl, :], 1, 0), pltpu.roll(cur, 1, 0))
        tail_ref[sl, :] = cur
    xr_all = conv_b_ref[...]
    for k in range(A_CONV):
        xr_all = xr_all + conv_w_ref[k:k + 1, :] * ext_ref[k * SUBLANES:k * SUBLANES + rows, :]
    xr_ref[...] = xr_all
    xr_bf16 = xr_all.astype(BF16)
    row8 = lax.broadcasted_iota(jnp.int32, (SUBLANES, A_BAND), 0)

    ys = []
    for band, lo in enumerate(window_starts):
        cols = slice(band * A_BAND, (band + 1) * A_BAND)
        xr = xr_ref[:, cols]
        ra = _dot(xr_bf16[:, lo:lo + A_WINDOW], w_band_s[band])
        r = jax.nn.sigmoid(ra[:, :A_BAND] + b_a_ref[:, cols])
        ig = jax.nn.sigmoid(ra[:, A_BAND:] + b_x_ref[:, cols])
        lam = lam_ref[:, cols]
        softplus_neg_lam = jnp.maximum(-lam, 0.0) + jnp.log1p(jnp.exp(-jnp.abs(lam)))
        a = jnp.exp2(r * ((-LRU_C * LOG2_E) * softplus_neg_lam))
        a_ref[:, cols] = a
        z = 1.0 - a * a
        u_ref[:, cols] = jnp.where(z > 0.0, z * lax.rsqrt(z), 0.0) * (ig * xr)

        def step(k):
            sl = slice(k * SUBLANES, (k + 1) * SUBLANES)
            return a_ref[sl, cols], u_ref[sl, cols], sl

        prod = jnp.ones((SUBLANES, A_BAND), F32)
        h_loc = jnp.zeros((SUBLANES, A_BAND), F32)
        for k in range(seg):
            a_k, u_k, _ = step(k)
            h_loc = a_k * h_loc + u_k
            prod = a_k * prod
        c_in = carry_ref[:, cols]
        cum_a, cum_b = _sublane_scan(prod, h_loc)
        seg_end = cum_a * c_in + cum_b
        carry_ref[:, cols] = jnp.broadcast_to(seg_end[SUBLANES - 1:SUBLANES, :], seg_end.shape)
        h_cur = jnp.where(row8 == 0, c_in, pltpu.roll(seg_end, 1, 0))
        for k in range(seg):
            a_k, u_k, sl = step(k)
            h_cur = a_k * h_cur + u_k
            u_ref[sl, cols] = h_cur

        gate = jax.nn.gelu(proj[:, D_RNN + band * A_BAND:D_RNN + (band + 1) * A_BAND])
        ys.append((u_ref[:, cols] * gate).astype(BF16))

    y = _dot(_segment_permutation(rows, True), jnp.concatenate(ys, axis=1)).astype(BF16)
    return x + _dot(y, w_out_s[...])


def _rglru_layer(x, i, j, g, w_in, conv_w, conv_b, gate_a_w, gate_a_b, gate_x_w, gate_x_b, lam, w_out,
                 mlp_w1, mlp_w2):
    lead = (A_CONV - 1) * SUBLANES
    n_sub = TS_MIX // SUB_ROWS
    head_tile = np.zeros((A_BANDS, A_HEAD_DIM, A_BAND), np.float32)
    head_mask = np.zeros((A_BANDS, A_WINDOW, A_BAND), np.float32)
    for band, lo in enumerate(_band_window_starts()):
        col = band * A_BAND + np.arange(A_BAND)
        row = lo + np.arange(A_WINDOW)
        head_tile[band] = col[None, :] % A_HEAD_DIM == np.arange(A_HEAD_DIM)[:, None]
        head_mask[band] = row[:, None] // A_HEAD_DIM == col[None, :] // A_HEAD_DIM
    n_a = w_in.shape[0]
    consts = [
        (_row(g), i), (w_in, j), (conv_w, j), (_row(conv_b), j),
        (gate_a_w.reshape(n_a, D_RNN, A_HEAD_DIM), j), (gate_x_w.reshape(n_a, D_RNN, A_HEAD_DIM), j),
        (jnp.asarray(head_tile, BF16)[None], 0), (jnp.asarray(head_mask, F32)[None], 0),
        (_row(gate_a_b), j), (_row(gate_x_b), j), (_row(lam), j), (w_out, j),
    ]
    scratch = [
        pltpu.VMEM((D_MODEL, 2 * D_RNN), BF16),
        pltpu.VMEM((A_BANDS, A_WINDOW, 2 * A_BAND), BF16),
        pltpu.VMEM((D_RNN, D_MODEL), BF16),
        pltpu.VMEM((n_sub, lead + SUB_ROWS, D_RNN), F32),
        pltpu.VMEM((lead, D_RNN), F32),
        pltpu.VMEM((n_sub, SUB_ROWS, D_RNN), F32),
        pltpu.VMEM((n_sub, SUB_ROWS, D_RNN), F32),
        pltpu.VMEM((n_sub, SUB_ROWS, D_RNN), F32),
        pltpu.VMEM((SUBLANES, D_RNN), F32),
    ]
    return _mixer_call(_rglru_body, "rglru_mixer", x, consts, scratch, mlp_w1, mlp_w2, i)


def _sgu_body(x_ref, g_ref, w_in_ref, ng_ref, w_s_ref, bias_ref, w_out_ref, o_ref,
              w_u_s, w_v_s, w_c_s, w_out_s, mix_ref):
    rows = x_ref.shape[1]

    @pl.when(_is_first_step())
    def _():
        w_u_s[...] = w_in_ref[:, :D_SGU].astype(BF16)
        w_v_s[...] = w_in_ref[:, D_SGU:].astype(BF16)
        w_out_s[...] = w_out_ref[...].astype(BF16)
        t_idx = lax.broadcasted_iota(jnp.int32, (SGU_CHUNK, SGU_CHUNK), 0)
        s_idx = lax.broadcasted_iota(jnp.int32, (SGU_CHUNK, SGU_CHUNK), 1)
        for grp in range(SGU_GROUPS):
            w_c_s[grp] = jnp.where(t_idx >= s_idx, w_s_ref[grp], 0.0).astype(BF16)

    for r0 in range(0, rows, SUB_ROWS):
        x = x_ref[0, r0:r0 + SUB_ROWS, :]
        h = _rmsnorm(x, g_ref[...]).astype(BF16)
        v = jax.nn.gelu(_dot(h, w_v_s[...]))
        v = _rmsnorm(v, ng_ref[...]).astype(BF16)
        u = jax.nn.gelu(_dot(h, w_u_s[...]))
        for grp in range(SGU_GROUPS):
            lanes = slice(grp * SGU_GROUP_DIM, (grp + 1) * SGU_GROUP_DIM)
            for c0 in range(0, SUB_ROWS, SGU_CHUNK):
                mix_ref[r0 + c0:r0 + c0 + SGU_CHUNK, lanes] = (
                    _dot(w_c_s[grp], v[c0:c0 + SGU_CHUNK, lanes]) + bias_ref[:, lanes])
        y = (u * mix_ref[r0:r0 + SUB_ROWS, :]).astype(BF16)
        o_ref[0, r0:r0 + SUB_ROWS, :] = x + _dot(y, w_out_s[...])


def _sgu_layer(x, i, j, g, w_in, norm_g, w_s, s_bias, w_out, mlp_w1, mlp_w2):
    bias = jnp.repeat(jnp.swapaxes(s_bias, 1, 2), SGU_GROUP_DIM, axis=2)
    consts = [(_row(g), i), (w_in, j), (_row(norm_g), j), (w_s, j), (bias, j), (w_out, j)]
    scratch = [
        pltpu.VMEM((D_MODEL, D_SGU), BF16),
        pltpu.VMEM((D_MODEL, D_SGU), BF16),
        pltpu.VMEM((SGU_GROUPS, SGU_CHUNK, SGU_CHUNK), BF16),
        pltpu.VMEM((D_SGU, D_MODEL), BF16),
        pltpu.VMEM((TS_MIX, D_SGU), F32),
    ]
    return _mixer_call(_sgu_body, "sgu_mixer", x, consts, scratch, mlp_w1, mlp_w2, i)


def _shortconv_body(x_ref, g_ref, w_in_ref, conv_w_ref, w_out_ref, o_ref, w_in_s, w_out_s, hist_ref):
    rows = x_ref.shape[1]

    @pl.when(_is_first_step())
    def _():
        w_in_s[...] = w_in_ref[...].astype(BF16)
        w_out_s[...] = w_out_ref[...].astype(BF16)

    @pl.when(pl.program_id(1) == 0)
    def _():
        hist_ref[0:HIST, :] = jnp.zeros((HIST, D_CONV), F32)

    for r0 in range(0, rows, SUB_ROWS):
        x = x_ref[0, r0:r0 + SUB_ROWS, :]
        h = _rmsnorm(x, g_ref[...]).astype(BF16)
        gb = _dot(h, w_in_s[:, :D_CONV])
        gc = _dot(h, w_in_s[:, D_CONV:2 * D_CONV])
        xv = _dot(h, w_in_s[:, 2 * D_CONV:])
        hist_ref[HIST + r0:HIST + r0 + SUB_ROWS, :] = gc * xv
        conv = None
        for k in range(C_CONV):
            start = HIST + r0 - (C_CONV - 1) + k
            term = conv_w_ref[k:k + 1, :] * hist_ref[start:start + SUB_ROWS, :]
            conv = term if conv is None else conv + term
        y = (gb * conv).astype(BF16)
        o_ref[0, r0:r0 + SUB_ROWS, :] = x + _dot(y, w_out_s[...])
    hist_ref[0:HIST, :] = hist_ref[rows:rows + HIST, :]


def _shortconv_layer(x, i, j, g, w_in, conv_w, w_out, mlp_w1, mlp_w2):
    consts = [(_row(g), i), (w_in, j), (conv_w, j), (w_out, j)]
    scratch = [
        pltpu.VMEM((D_MODEL, 3 * D_CONV), BF16),
        pltpu.VMEM((D_CONV, D_MODEL), BF16),
        pltpu.VMEM((HIST + TS_MIX, D_CONV), F32),
    ]
    return _mixer_call(_shortconv_body, "shortconv_mixer", x, consts, scratch, mlp_w1, mlp_w2, i)


def _mlp_kernel(x_ref, g_ref, w1_ref, w2_ref, fg_ref, o_ref, *, final_norm):
    x = x_ref[...]
    h = _rmsnorm(x, g_ref[...]).astype(BF16)
    acc = x
    for c in range(D_FF // FF_CHUNK):
        cols = slice(c * FF_CHUNK, (c + 1) * FF_CHUNK)
        a = jnp.square(jnp.maximum(_dot(h, w1_ref[:, cols]), 0.0)).astype(BF16)
        acc = acc + _dot(a, w2_ref[cols, :])
    if final_norm:
        acc = _rmsnorm(acc, fg_ref[...])
    o_ref[...] = acc


def _mlp_layer(x2d, i, g, w1_bf16, w2_bf16, final_g, final_norm):
    n, _ = x2d.shape
    tm = TM_MLP
    row_spec = pl.BlockSpec((tm, D_MODEL), lambda r: (r, 0))
    return pl.pallas_call(
        functools.partial(_mlp_kernel, final_norm=final_norm),
        grid=(n // tm,),
        in_specs=[
            row_spec,
            _layer_spec(_row(g), i),
            _layer_spec(w1_bf16[None], 0),
            _layer_spec(w2_bf16[None], 0),
            _layer_spec(final_g[None, None, :], 0),
        ],
        out_specs=row_spec,
        out_shape=jax.ShapeDtypeStruct(x2d.shape, F32),
        compiler_params=pltpu.CompilerParams(
            dimension_semantics=("arbitrary",), vmem_limit_bytes=VMEM_LIMIT),
        name="sqrelu_mlp",
    )(x2d, _row(g), w1_bf16[None], w2_bf16[None], final_g[None, None, :])


def kernel(x, norm_mix_g, norm_mlp_g, final_norm_g, a_w_in, a_conv_w, a_conv_b, a_gate_a_w, a_gate_a_b, a_gate_x_w, a_gate_x_b, a_lambda, a_w_out, b_w_in, b_norm_g, b_w_s, b_s_bias, b_w_out, c_w_in, c_conv_w, c_w_out, mlp_w1, mlp_w2):
    b, s, d = x.shape
    assert s % TS_MIX == 0 and (b * s) % TM_MLP == 0
    assert TS_MIX % SUB_ROWS == 0 and SUB_ROWS % SGU_CHUNK == 0
    for i in range(DEPTH):
        kind, j = i % N_MIXERS, i // N_MIXERS
        if kind == 0:
            x, w1, w2 = _rglru_layer(x, i, j, norm_mix_g, a_w_in, a_conv_w, a_conv_b, a_gate_a_w,
                                     a_gate_a_b, a_gate_x_w, a_gate_x_b, a_lambda, a_w_out,
                                     mlp_w1, mlp_w2)
        elif kind == 1:
            x, w1, w2 = _sgu_layer(x, i, j, norm_mix_g, b_w_in, b_norm_g, b_w_s, b_s_bias, b_w_out,
                                   mlp_w1, mlp_w2)
        else:
            x, w1, w2 = _shortconv_layer(x, i, j, norm_mix_g, c_w_in, c_conv_w, c_w_out, mlp_w1, mlp_w2)
        x = _mlp_layer(x.reshape(b * s, d), i, norm_mlp_g, w1, w2, final_norm_g,
                       final_norm=(i == DEPTH - 1)).reshape(b, s, d)
    return x
```

```python
import functools

import jax
import jax.numpy as jnp
import numpy as np
from jax import lax
from jax.experimental import pallas as pl
from jax.experimental.pallas import tpu as pltpu

D_MODEL = 1024
DEPTH = 4
N_MIXERS = 3
D_RNN = 1280
A_HEADS = 16
A_HEAD_DIM = D_RNN // A_HEADS
A_CONV = 4
LRU_C = 8.0
D_SGU = D_MODEL
SGU_CHUNK = 128
SGU_GROUPS = 8
SGU_GROUP_DIM = D_SGU // SGU_GROUPS
D_CONV = D_MODEL
C_CONV = 3
D_FF = 4 * D_MODEL
EPS = 1e-6
LOG2_E = 1.4426950408889634

SUBLANES = 8
BF16_ROWS = 16
LANES = 128
MXU_COLS = 256
COL_TILE = 2 * MXU_COLS
A_BAND = MXU_COLS
A_BANDS = D_RNN // A_BAND
A_WINDOW = 2 * A_BAND
HIST = SUBLANES
TS_MIX = 512
SUB_ROWS = 256
TM_MLP = 512
FF_CHUNK = 2048
VMEM_LIMIT = 56 * 1024 * 1024

F32 = jnp.float32
BF16 = jnp.bfloat16


def _rmsnorm(x, g):
    return x * lax.rsqrt(jnp.mean(x * x, axis=-1, keepdims=True) + EPS) * g


def _dot(a, b):
    return jnp.dot(a, b, preferred_element_type=F32)


def _layer_spec(stacked, layer):
    zeros = (0,) * (stacked.ndim - 1)
    return pl.BlockSpec((None,) + stacked.shape[1:], lambda *_: (layer,) + zeros,
                        pipeline_mode=pl.Buffered(1))


def _row(stacked):
    return stacked[:, None, :]


def _is_first_step():
    return jnp.logical_and(pl.program_id(0) == 0, pl.program_id(1) == 0)


def _mixer_call(body, name, x, consts, scratch_shapes, mlp_w1, mlp_w2, mlp_layer):
    b, s, _ = x.shape
    nt = s // TS_MIX
    steps = b * nt
    row_spec = pl.BlockSpec((1, TS_MIX, D_MODEL), lambda i, j: (i, j, 0))

    def slice_rows(w):
        rows = w.shape[1] // steps
        assert rows * steps == w.shape[1] and rows % BF16_ROWS == 0
        return rows

    def in_slice_spec(w):
        return pl.BlockSpec((None, slice_rows(w), w.shape[2]), lambda i, j: (mlp_layer, i * nt + j, 0))

    def out_slice_spec(w):
        return pl.BlockSpec((slice_rows(w), w.shape[2]), lambda i, j: (i * nt + j, 0))

    n_const = len(consts)

    def kernel(x_ref, *refs):
        const_refs = refs[:n_const]
        w1_ref, w2_ref, o_ref, w1_out_ref, w2_out_ref = refs[n_const:n_const + 5]
        w1_out_ref[...] = w1_ref[...].astype(BF16)
        w2_out_ref[...] = w2_ref[...].astype(BF16)
        body(x_ref, *const_refs, o_ref, *refs[n_const + 5:])

    return pl.pallas_call(
        kernel,
        grid=(b, nt),
        in_specs=[row_spec] + [_layer_spec(arr, layer) for arr, layer in consts]
        + [in_slice_spec(mlp_w1), in_slice_spec(mlp_w2)],
        out_specs=[row_spec, out_slice_spec(mlp_w1), out_slice_spec(mlp_w2)],
        out_shape=[jax.ShapeDtypeStruct(x.shape, F32),
                   jax.ShapeDtypeStruct(mlp_w1.shape[1:], BF16),
                   jax.ShapeDtypeStruct(mlp_w2.shape[1:], BF16)],
        scratch_shapes=scratch_shapes,
        compiler_params=pltpu.CompilerParams(
            dimension_semantics=("arbitrary", "arbitrary"), vmem_limit_bytes=VMEM_LIMIT),
        name=name,
    )(x, *[arr for arr, _ in consts], mlp_w1, mlp_w2)


def _sublane_scan(a, b):
    row = lax.broadcasted_iota(jnp.int32, a.shape, 0)
    for d in (1, 2, 4):
        keep = row >= d
        a_prev = jnp.where(keep, pltpu.roll(a, d, 0), 1.0)
        b_prev = jnp.where(keep, pltpu.roll(b, d, 0), 0.0)
        b = a * b_prev + b
        a = a * a_prev
    return a, b


def _segment_permutation(rows, inverse):
    seg = rows // SUBLANES
    i0 = lax.broadcasted_iota(jnp.int32, (rows, rows), 0)
    i1 = lax.broadcasted_iota(jnp.int32, (rows, rows), 1)
    major, natural = (i1, i0) if inverse else (i0, i1)
    hit = natural == (major % SUBLANES) * seg + major // SUBLANES
    return jnp.where(hit, 1.0, 0.0).astype(BF16)


def _band_window_starts():
    starts = []
    for band in range(A_BANDS):
        first_head = (band * A_BAND) // A_HEAD_DIM
        last_head = -(-((band + 1) * A_BAND) // A_HEAD_DIM)
        lo = min((first_head * A_HEAD_DIM) // LANES * LANES, D_RNN - A_WINDOW)
        assert lo <= first_head * A_HEAD_DIM and last_head * A_HEAD_DIM <= lo + A_WINDOW
        starts.append(lo)
    return starts


def _rglru_body(x_ref, g_ref, w_in_ref, conv_w_ref, conv_b_ref, gate_a_ref, gate_x_ref,
                head_tile_ref, head_mask_ref, b_a_ref, b_x_ref, lam_ref, w_out_ref, o_ref,
                w_in_s, w_band_s, w_out_s, ext_ref, tail_ref, xr_ref, a_ref, u_ref, carry_ref):
    window_starts = _band_window_starts()

    @pl.when(_is_first_step())
    def _():
        w_in_s[:, :D_RNN] = w_in_ref[:, D_RNN:].astype(BF16)
        w_in_s[:, D_RNN:] = w_in_ref[:, :D_RNN].astype(BF16)
        w_out_s[...] = w_out_ref[...].astype(BF16)
        for band, lo in enumerate(window_starts):
            for part, gate_ref in enumerate((gate_a_ref, gate_x_ref)):
                tiled = _dot(gate_ref[lo:lo + A_WINDOW, :].astype(BF16), head_tile_ref[band])
                w_band_s[band, :, part * A_BAND:(part + 1) * A_BAND] = (
                    tiled * head_mask_ref[band]).astype(BF16)

    @pl.when(pl.program_id(1) == 0)
    def _():
        tail_ref[...] = jnp.zeros(tail_ref.shape, F32)
        carry_ref[...] = jnp.zeros(carry_ref.shape, F32)

    n_sub = x_ref.shape[1] // SUB_ROWS

    def front(sub):
        x = x_ref[0, sub * SUB_ROWS:(sub + 1) * SUB_ROWS, :]
        return _rglru_front(x, g_ref, conv_w_ref, conv_b_ref, w_in_s,
                            ext_ref.at[sub], tail_ref, xr_ref.at[sub])

    state, _ = _interleave(front(0), None)
    for sub in range(n_sub):
        back = _rglru_back(*state, b_a_ref, b_x_ref, lam_ref, w_band_s, w_out_s,
                           xr_ref.at[sub], a_ref.at[sub], u_ref.at[sub], carry_ref)
        out, state = _interleave(back, front(sub + 1) if sub + 1 < n_sub else None)
        o_ref[0, sub * SUB_ROWS:(sub + 1) * SUB_ROWS, :] = out


def _interleave(first, second):
    results = [None, None]
    live = {0: first, 1: second}
    live = {k: g for k, g in live.items() if g is not None}
    while live:
        for k in list(live):
            try:
                next(live[k])
            except StopIteration as done:
                results[k] = done.value
                del live[k]
    return results


def _rglru_front(x, g_ref, conv_w_ref, conv_b_ref, w_in_s, ext_ref, tail_ref, xr_ref):
    rows = SUB_ROWS
    lead = (A_CONV - 1) * SUBLANES
    h = _rmsnorm(x, g_ref[...]).astype(BF16)
    hp = _dot(_segment_permutation(rows, False), h).astype(BF16)
    chunks = []
    for c0 in range(0, 2 * D_RNN, COL_TILE):
        chunks.append(_dot(hp, w_in_s[:, c0:c0 + COL_TILE]))
        yield
    proj = jnp.concatenate(chunks, axis=1)
    ext_ref[lead:lead + rows, :] = proj[:, :D_RNN]
    row8 = lax.broadcasted_iota(jnp.int32, (SUBLANES, D_RNN), 0)
    for j in range(A_CONV - 1):
        sl = slice(j * SUBLANES, (j + 1) * SUBLANES)
        cur = ext_ref[rows + j * SUBLANES:rows + (j + 1) * SUBLANES, :]
        ext_ref[sl, :] = jnp.where(row8 == 0, pltpu.roll(tail_ref[sl, :], 1, 0), pltpu.roll(cur, 1, 0))
        tail_ref[sl, :] = cur
    xr_all = conv_b_ref[...]
    for k in range(A_CONV):
        xr_all = xr_all + conv_w_ref[k:k + 1, :] * ext_ref[k * SUBLANES:k * SUBLANES + rows, :]
    xr_ref[...] = xr_all
    return x, xr_all.astype(BF16), proj[:, D_RNN:]


def _rglru_back(x, xr_bf16, gate_pre, b_a_ref, b_x_ref, lam_ref, w_band_s, w_out_s,
                xr_ref, a_ref, u_ref, carry_ref):
    rows = SUB_ROWS
    seg = rows // SUBLANES
    window_starts = _band_window_starts()
    row8 = lax.broadcasted_iota(jnp.int32, (SUBLANES, A_BAND), 0)

    ys = []
    for band, lo in enumerate(window_starts):
        cols = slice(band * A_BAND, (band + 1) * A_BAND)
        xr = xr_ref[:, cols]
        ra = _dot(xr_bf16[:, lo:lo + A_WINDOW], w_band_s[band])
        r = jax.nn.sigmoid(ra[:, :A_BAND] + b_a_ref[:, cols])
        ig = jax.nn.sigmoid(ra[:, A_BAND:] + b_x_ref[:, cols])
        lam = lam_ref[:, cols]
        softplus_neg_lam = jnp.maximum(-lam, 0.0) + jnp.log1p(jnp.exp(-jnp.abs(lam)))
        a = jnp.exp2(r * ((-LRU_C * LOG2_E) * softplus_neg_lam))
        a_ref[:, cols] = a
        z = 1.0 - a * a
        u_ref[:, cols] = jnp.where(z > 0.0, z * lax.rsqrt(z), 0.0) * (ig * xr)

        def step(k):
            sl = slice(k * SUBLANES, (k + 1) * SUBLANES)
            return a_ref[sl, cols], u_ref[sl, cols], sl

        prod = jnp.ones((SUBLANES, A_BAND), F32)
        h_loc = jnp.zeros((SUBLANES, A_BAND), F32)
        for k in range(seg):
            a_k, u_k, _ = step(k)
            h_loc = a_k * h_loc + u_k
            prod = a_k * prod
        c_in = carry_ref[:, cols]
        cum_a, cum_b = _sublane_scan(prod, h_loc)
        seg_end = cum_a * c_in + cum_b
        carry_ref[:, cols] = jnp.broadcast_to(seg_end[SUBLANES - 1:SUBLANES, :], seg_end.shape)
        h_cur = jnp.where(row8 == 0, c_in, pltpu.roll(seg_end, 1, 0))
        for k in range(seg):
            a_k, u_k, sl = step(k)
            h_cur = a_k * h_cur + u_k
            u_ref[sl, cols] = h_cur

        gate = jax.nn.gelu(gate_pre[:, cols])
        ys.append((u_ref[:, cols] * gate).astype(BF16))
        yield

    y = _dot(_segment_permutation(rows, True), jnp.concatenate(ys, axis=1)).astype(BF16)
    return x + _dot(y, w_out_s[...])


def _rglru_layer(x, i, j, g, w_in, conv_w, conv_b, gate_a_w, gate_a_b, gate_x_w, gate_x_b, lam, w_out,
                 mlp_w1, mlp_w2):
    lead = (A_CONV - 1) * SUBLANES
    n_sub = TS_MIX // SUB_ROWS
    head_tile = np.zeros((A_BANDS, A_HEAD_DIM, A_BAND), np.float32)
    head_mask = np.zeros((A_BANDS, A_WINDOW, A_BAND), np.float32)
    for band, lo in enumerate(_band_window_starts()):
        col = band * A_BAND + np.arange(A_BAND)
        row = lo + np.arange(A_WINDOW)
        head_tile[band] = col[None, :] % A_HEAD_DIM == np.arange(A_HEAD_DIM)[:, None]
        head_mask[band] = row[:, None] // A_HEAD_DIM == col[None, :] // A_HEAD_DIM
    n_a = w_in.shape[0]
    consts = [
        (_row(g), i), (w_in, j), (conv_w, j), (_row(conv_b), j),
        (gate_a_w.reshape(n_a, D_RNN, A_HEAD_DIM), j), (gate_x_w.reshape(n_a, D_RNN, A_HEAD_DIM), j),
        (jnp.asarray(head_tile, BF16)[None], 0), (jnp.asarray(head_mask, F32)[None], 0),
        (_row(gate_a_b), j), (_row(gate_x_b), j), (_row(lam), j), (w_out, j),
    ]
    scratch = [
        pltpu.VMEM((D_MODEL, 2 * D_RNN), BF16),
        pltpu.VMEM((A_BANDS, A_WINDOW, 2 * A_BAND), BF16),
        pltpu.VMEM((D_RNN, D_MODEL), BF16),
        pltpu.VMEM((n_sub, lead + SUB_ROWS, D_RNN), F32),
        pltpu.VMEM((lead, D_RNN), F32),
        pltpu.VMEM((n_sub, SUB_ROWS, D_RNN), F32),
        pltpu.VMEM((n_sub, SUB_ROWS, D_RNN), F32),
        pltpu.VMEM((n_sub, SUB_ROWS, D_RNN), F32),
        pltpu.VMEM((SUBLANES, D_RNN), F32),
    ]
    return _mixer_call(_rglru_body, "rglru_mixer", x, consts, scratch, mlp_w1, mlp_w2, i)


def _sgu_body(x_ref, g_ref, w_in_ref, ng_ref, w_s_ref, bias_ref, w_out_ref, o_ref,
              w_u_s, w_v_s, w_c_s, w_out_s, mix_ref):
    rows = x_ref.shape[1]

    @pl.when(_is_first_step())
    def _():
        w_u_s[...] = w_in_ref[:, :D_SGU].astype(BF16)
        w_v_s[...] = w_in_ref[:, D_SGU:].astype(BF16)
        w_out_s[...] = w_out_ref[...].astype(BF16)
        t_idx = lax.broadcasted_iota(jnp.int32, (SGU_CHUNK, SGU_CHUNK), 0)
        s_idx = lax.broadcasted_iota(jnp.int32, (SGU_CHUNK, SGU_CHUNK), 1)
        for grp in range(SGU_GROUPS):
            w_c_s[grp] = jnp.where(t_idx >= s_idx, w_s_ref[grp], 0.0).astype(BF16)

    for r0 in range(0, rows, SUB_ROWS):
        x = x_ref[0, r0:r0 + SUB_ROWS, :]
        h = _rmsnorm(x, g_ref[...]).astype(BF16)
        v = jax.nn.gelu(_dot(h, w_v_s[...]))
        v = _rmsnorm(v, ng_ref[...]).astype(BF16)
        u = jax.nn.gelu(_dot(h, w_u_s[...]))
        for grp in range(SGU_GROUPS):
            lanes = slice(grp * SGU_GROUP_DIM, (grp + 1) * SGU_GROUP_DIM)
            for c0 in range(0, SUB_ROWS, SGU_CHUNK):
                mix_ref[r0 + c0:r0 + c0 + SGU_CHUNK, lanes] = (
                    _dot(w_c_s[grp], v[c0:c0 + SGU_CHUNK, lanes]) + bias_ref[:, lanes])
        y = (u * mix_ref[r0:r0 + SUB_ROWS, :]).astype(BF16)
        o_ref[0, r0:r0 + SUB_ROWS, :] = x + _dot(y, w_out_s[...])


def _sgu_layer(x, i, j, g, w_in, norm_g, w_s, s_bias, w_out, mlp_w1, mlp_w2):
    bias = jnp.repeat(jnp.swapaxes(s_bias, 1, 2), SGU_GROUP_DIM, axis=2)
    consts = [(_row(g), i), (w_in, j), (_row(norm_g), j), (w_s, j), (bias, j), (w_out, j)]
    scratch = [
        pltpu.VMEM((D_MODEL, D_SGU), BF16),
        pltpu.VMEM((D_MODEL, D_SGU), BF16),
        pltpu.VMEM((SGU_GROUPS, SGU_CHUNK, SGU_CHUNK), BF16),
        pltpu.VMEM((D_SGU, D_MODEL), BF16),
        pltpu.VMEM((TS_MIX, D_SGU), F32),
    ]
    return _mixer_call(_sgu_body, "sgu_mixer", x, consts, scratch, mlp_w1, mlp_w2, i)


def _shortconv_body(x_ref, g_ref, w_in_ref, conv_w_ref, w_out_ref, o_ref, w_in_s, w_out_s, hist_ref):
    rows = x_ref.shape[1]

    @pl.when(_is_first_step())
    def _():
        w_in_s[...] = w_in_ref[...].astype(BF16)
        w_out_s[...] = w_out_ref[...].astype(BF16)

    @pl.when(pl.program_id(1) == 0)
    def _():
        hist_ref[0:HIST, :] = jnp.zeros((HIST, D_CONV), F32)

    for r0 in range(0, rows, SUB_ROWS):
        x = x_ref[0, r0:r0 + SUB_ROWS, :]
        h = _rmsnorm(x, g_ref[...]).astype(BF16)
        gb = _dot(h, w_in_s[:, :D_CONV])
        gc = _dot(h, w_in_s[:, D_CONV:2 * D_CONV])
        xv = _dot(h, w_in_s[:, 2 * D_CONV:])
        hist_ref[HIST + r0:HIST + r0 + SUB_ROWS, :] = gc * xv
        conv = None
        for k in range(C_CONV):
            start = HIST + r0 - (C_CONV - 1) + k
            term = conv_w_ref[k:k + 1, :] * hist_ref[start:start + SUB_ROWS, :]
            conv = term if conv is None else conv + term
        y = (gb * conv).astype(BF16)
        o_ref[0, r0:r0 + SUB_ROWS, :] = x + _dot(y, w_out_s[...])
    hist_ref[0:HIST, :] = hist_ref[rows:rows + HIST, :]


def _shortconv_layer(x, i, j, g, w_in, conv_w, w_out, mlp_w1, mlp_w2):
    consts = [(_row(g), i), (w_in, j), (conv_w, j), (w_out, j)]
    scratch = [
        pltpu.VMEM((D_MODEL, 3 * D_CONV), BF16),
        pltpu.VMEM((D_CONV, D_MODEL), BF16),
        pltpu.VMEM((HIST + TS_MIX, D_CONV), F32),
    ]
    return _mixer_call(_shortconv_body, "shortconv_mixer", x, consts, scratch, mlp_w1, mlp_w2, i)


def _mlp_kernel(x_ref, g_ref, w1_ref, w2_ref, fg_ref, o_ref, *, final_norm):
    x = x_ref[...]
    h = _rmsnorm(x, g_ref[...]).astype(BF16)
    acc = x
    for c in range(D_FF // FF_CHUNK):
        cols = slice(c * FF_CHUNK, (c + 1) * FF_CHUNK)
        a = jnp.square(jnp.maximum(_dot(h, w1_ref[:, cols]), 0.0)).astype(BF16)
        acc = acc + _dot(a, w2_ref[cols, :])
    if final_norm:
        acc = _rmsnorm(acc, fg_ref[...])
    o_ref[...] = acc


def _mlp_layer(x2d, i, g, w1_bf16, w2_bf16, final_g, final_norm):
    n, _ = x2d.shape
    tm = TM_MLP
    row_spec = pl.BlockSpec((tm, D_MODEL), lambda r: (r, 0))
    return pl.pallas_call(
        functools.partial(_mlp_kernel, final_norm=final_norm),
        grid=(n // tm,),
        in_specs=[
            row_spec,
            _layer_spec(_row(g), i),
            _layer_spec(w1_bf16[None], 0),
            _layer_spec(w2_bf16[None], 0),
            _layer_spec(final_g[None, None, :], 0),
        ],
        out_specs=row_spec,
        out_shape=jax.ShapeDtypeStruct(x2d.shape, F32),
        compiler_params=pltpu.CompilerParams(
            dimension_semantics=("arbitrary",), vmem_limit_bytes=VMEM_LIMIT),
        name="sqrelu_mlp",
    )(x2d, _row(g), w1_bf16[None], w2_bf16[None], final_g[None, None, :])


def kernel(x, norm_mix_g, norm_mlp_g, final_norm_g, a_w_in, a_conv_w, a_conv_b, a_gate_a_w, a_gate_a_b, a_gate_x_w, a_gate_x_b, a_lambda, a_w_out, b_w_in, b_norm_g, b_w_s, b_s_bias, b_w_out, c_w_in, c_conv_w, c_w_out, mlp_w1, mlp_w2):
    b, s, d = x.shape
    assert s % TS_MIX == 0 and (b * s) % TM_MLP == 0
    assert TS_MIX % SUB_ROWS == 0 and SUB_ROWS % SGU_CHUNK == 0
    for i in range(DEPTH):
        kind, j = i % N_MIXERS, i // N_MIXERS
        if kind == 0:
            x, w1, w2 = _rglru_layer(x, i, j, norm_mix_g, a_w_in, a_conv_w, a_conv_b, a_gate_a_w,
                                     a_gate_a_b, a_gate_x_w, a_gate_x_b, a_lambda, a_w_out,
                                     mlp_w1, mlp_w2)
        elif kind == 1:
            x, w1, w2 = _sgu_layer(x, i, j, norm_mix_g, b_w_in, b_norm_g, b_w_s, b_s_bias, b_w_out,
                                   mlp_w1, mlp_w2)
        else:
            x, w1, w2 = _shortconv_layer(x, i, j, norm_mix_g, c_w_in, c_conv_w, c_w_out, mlp_w1, mlp_w2)
        x = _mlp_layer(x.reshape(b * s, d), i, norm_mlp_g, w1, w2, final_norm_g,
                       final_norm=(i == DEPTH - 1)).reshape(b, s, d)
    return x
```

```python
import functools

import jax
import jax.numpy as jnp
import numpy as np
from jax import lax
from jax.experimental import pallas as pl
from jax.experimental.pallas import tpu as pltpu

D_MODEL = 1024
DEPTH = 4
N_MIXERS = 3
D_RNN = 1280
A_HEADS = 16
A_HEAD_DIM = D_RNN // A_HEADS
A_CONV = 4
LRU_C = 8.0
D_SGU = D_MODEL
SGU_CHUNK = 128
SGU_GROUPS = 8
SGU_GROUP_DIM = D_SGU // SGU_GROUPS
D_CONV = D_MODEL
C_CONV = 3
D_FF = 4 * D_MODEL
EPS = 1e-6
LOG2_E = 1.4426950408889634

SUBLANES = 8
BF16_ROWS = 16
LANES = 128
MXU_COLS = 256
COL_TILE = 2 * MXU_COLS
A_BAND = MXU_COLS
A_BANDS = D_RNN // A_BAND
A_WINDOW = 2 * A_BAND
HIST = SUBLANES
TS_RGLRU = 512
TS_SGU = 1024
TS_SHORTCONV = 1024
SUB_ROWS = 256
SGU_SUB_ROWS = 512
TM_MLP = 1024
MLP_SUB_ROWS = 512
FF_CHUNK = 2048
VMEM_LIMIT = 56 * 1024 * 1024

F32 = jnp.float32
BF16 = jnp.bfloat16


def _rmsnorm(x, g):
    return x * lax.rsqrt(jnp.mean(x * x, axis=-1, keepdims=True) + EPS) * g


def _dot(a, b):
    return jnp.dot(a, b, preferred_element_type=F32)


def _layer_spec(stacked, layer):
    zeros = (0,) * (stacked.ndim - 1)
    return pl.BlockSpec((None,) + stacked.shape[1:], lambda *_: (layer,) + zeros,
                        pipeline_mode=pl.Buffered(1))


def _row(stacked):
    return stacked[:, None, :]


def _is_first_step():
    return jnp.logical_and(pl.program_id(0) == 0, pl.program_id(1) == 0)


def _mixer_call(body, name, ts, x, consts, scratch_shapes, mlp_w1, mlp_w2, mlp_layer):
    b, s, _ = x.shape
    assert s % ts == 0 and ts % SUB_ROWS == 0
    nt = s // ts
    steps = b * nt
    row_spec = pl.BlockSpec((1, ts, D_MODEL), lambda i, j: (i, j, 0))

    def slice_rows(w):
        rows = w.shape[1] // steps
        assert rows * steps == w.shape[1] and rows % BF16_ROWS == 0
        return rows

    def in_slice_spec(w):
        return pl.BlockSpec((None, slice_rows(w), w.shape[2]), lambda i, j: (mlp_layer, i * nt + j, 0))

    def out_slice_spec(w):
        return pl.BlockSpec((slice_rows(w), w.shape[2]), lambda i, j: (i * nt + j, 0))

    n_const = len(consts)

    def kernel(x_ref, *refs):
        const_refs = refs[:n_const]
        w1_ref, w2_ref, o_ref, w1_out_ref, w2_out_ref = refs[n_const:n_const + 5]
        w1_out_ref[...] = w1_ref[...].astype(BF16)
        w2_out_ref[...] = w2_ref[...].astype(BF16)
        body(x_ref, *const_refs, o_ref, *refs[n_const + 5:])

    return pl.pallas_call(
        kernel,
        grid=(b, nt),
        in_specs=[row_spec] + [_layer_spec(arr, layer) for arr, layer in consts]
        + [in_slice_spec(mlp_w1), in_slice_spec(mlp_w2)],
        out_specs=[row_spec, out_slice_spec(mlp_w1), out_slice_spec(mlp_w2)],
        out_shape=[jax.ShapeDtypeStruct(x.shape, F32),
                   jax.ShapeDtypeStruct(mlp_w1.shape[1:], BF16),
                   jax.ShapeDtypeStruct(mlp_w2.shape[1:], BF16)],
        scratch_shapes=scratch_shapes,
        compiler_params=pltpu.CompilerParams(
            dimension_semantics=("arbitrary", "arbitrary"), vmem_limit_bytes=VMEM_LIMIT),
        name=name,
    )(x, *[arr for arr, _ in consts], mlp_w1, mlp_w2)


def _sublane_scan(a, b):
    row = lax.broadcasted_iota(jnp.int32, a.shape, 0)
    for d in (1, 2, 4):
        keep = row >= d
        a_prev = jnp.where(keep, pltpu.roll(a, d, 0), 1.0)
        b_prev = jnp.where(keep, pltpu.roll(b, d, 0), 0.0)
        b = a * b_prev + b
        a = a * a_prev
    return a, b


def _segment_permutation(rows, inverse):
    seg = rows // SUBLANES
    i0 = lax.broadcasted_iota(jnp.int32, (rows, rows), 0)
    i1 = lax.broadcasted_iota(jnp.int32, (rows, rows), 1)
    major, natural = (i1, i0) if inverse else (i0, i1)
    hit = natural == (major % SUBLANES) * seg + major // SUBLANES
    return jnp.where(hit, 1.0, 0.0).astype(BF16)


def _band_window_starts():
    starts = []
    for band in range(A_BANDS):
        first_head = (band * A_BAND) // A_HEAD_DIM
        last_head = -(-((band + 1) * A_BAND) // A_HEAD_DIM)
        lo = min((first_head * A_HEAD_DIM) // LANES * LANES, D_RNN - A_WINDOW)
        assert lo <= first_head * A_HEAD_DIM and last_head * A_HEAD_DIM <= lo + A_WINDOW
        starts.append(lo)
    return starts


def _rglru_body(x_ref, g_ref, w_in_ref, conv_w_ref, conv_b_ref, gate_a_ref, gate_x_ref,
                head_tile_ref, head_mask_ref, b_a_ref, b_x_ref, lam_ref, w_out_ref, o_ref,
                w_in_s, w_band_s, w_out_s, ext_ref, tail_ref, xr_ref, a_ref, u_ref, carry_ref):
    window_starts = _band_window_starts()

    @pl.when(_is_first_step())
    def _():
        w_in_s[:, :D_RNN] = w_in_ref[:, D_RNN:].astype(BF16)
        w_in_s[:, D_RNN:] = w_in_ref[:, :D_RNN].astype(BF16)
        w_out_s[...] = w_out_ref[...].astype(BF16)
        for band, lo in enumerate(window_starts):
            for part, gate_ref in enumerate((gate_a_ref, gate_x_ref)):
                tiled = _dot(gate_ref[lo:lo + A_WINDOW, :].astype(BF16), head_tile_ref[band])
                w_band_s[band, :, part * A_BAND:(part + 1) * A_BAND] = (
                    tiled * head_mask_ref[band]).astype(BF16)

    @pl.when(pl.program_id(1) == 0)
    def _():
        tail_ref[...] = jnp.zeros(tail_ref.shape, F32)
        carry_ref[...] = jnp.zeros(carry_ref.shape, F32)

    n_sub = x_ref.shape[1] // SUB_ROWS

    def front(sub):
        x = x_ref[0, sub * SUB_ROWS:(sub + 1) * SUB_ROWS, :]
        return _rglru_front(x, g_ref, conv_w_ref, conv_b_ref, w_in_s,
                            ext_ref.at[sub], tail_ref, xr_ref.at[sub])

    state, _ = _interleave(front(0), None)
    for sub in range(n_sub):
        back = _rglru_back(*state, b_a_ref, b_x_ref, lam_ref, w_band_s, w_out_s,
                           xr_ref.at[sub], a_ref.at[sub], u_ref.at[sub], carry_ref)
        out, state = _interleave(back, front(sub + 1) if sub + 1 < n_sub else None)
        o_ref[0, sub * SUB_ROWS:(sub + 1) * SUB_ROWS, :] = out


def _interleave(first, second):
    results = [None, None]
    live = {0: first, 1: second}
    live = {k: g for k, g in live.items() if g is not None}
    while live:
        for k in list(live):
            try:
                next(live[k])
            except StopIteration as done:
                results[k] = done.value
                del live[k]
    return results


def _rglru_front(x, g_ref, conv_w_ref, conv_b_ref, w_in_s, ext_ref, tail_ref, xr_ref):
    rows = SUB_ROWS
    lead = (A_CONV - 1) * SUBLANES
    h = _rmsnorm(x, g_ref[...]).astype(BF16)
    hp = _dot(_segment_permutation(rows, False), h).astype(BF16)
    chunks = []
    for c0 in range(0, 2 * D_RNN, COL_TILE):
        chunks.append(_dot(hp, w_in_s[:, c0:c0 + COL_TILE]))
        yield
    proj = jnp.concatenate(chunks, axis=1)
    ext_ref[lead:lead + rows, :] = proj[:, :D_RNN]
    row8 = lax.broadcasted_iota(jnp.int32, (SUBLANES, D_RNN), 0)
    for j in range(A_CONV - 1):
        sl = slice(j * SUBLANES, (j + 1) * SUBLANES)
        cur = ext_ref[rows + j * SUBLANES:rows + (j + 1) * SUBLANES, :]
        ext_ref[sl, :] = jnp.where(row8 == 0, pltpu.roll(tail_ref[sl, :], 1, 0), pltpu.roll(cur, 1, 0))
        tail_ref[sl, :] = cur
    xr_all = conv_b_ref[...]
    for k in range(A_CONV):
        xr_all = xr_all + conv_w_ref[k:k + 1, :] * ext_ref[k * SUBLANES:k * SUBLANES + rows, :]
    xr_ref[...] = xr_all
    return x, xr_all.astype(BF16), proj[:, D_RNN:]


def _rglru_back(x, xr_bf16, gate_pre, b_a_ref, b_x_ref, lam_ref, w_band_s, w_out_s,
                xr_ref, a_ref, u_ref, carry_ref):
    rows = SUB_ROWS
    seg = rows // SUBLANES
    window_starts = _band_window_starts()
    row8 = lax.broadcasted_iota(jnp.int32, (SUBLANES, A_BAND), 0)

    ys = []
    for band, lo in enumerate(window_starts):
        cols = slice(band * A_BAND, (band + 1) * A_BAND)
        xr = xr_ref[:, cols]
        ra = _dot(xr_bf16[:, lo:lo + A_WINDOW], w_band_s[band])
        r = jax.nn.sigmoid(ra[:, :A_BAND] + b_a_ref[:, cols])
        ig = jax.nn.sigmoid(ra[:, A_BAND:] + b_x_ref[:, cols])
        lam = lam_ref[:, cols]
        softplus_neg_lam = jnp.maximum(-lam, 0.0) + jnp.log1p(jnp.exp(-jnp.abs(lam)))
        a = jnp.exp2(r * ((-LRU_C * LOG2_E) * softplus_neg_lam))
        a_ref[:, cols] = a
        z = 1.0 - a * a
        u_ref[:, cols] = jnp.where(z > 0.0, z * lax.rsqrt(z), 0.0) * (ig * xr)

        def step(k):
            sl = slice(k * SUBLANES, (k + 1) * SUBLANES)
            return a_ref[sl, cols], u_ref[sl, cols], sl

        prod = jnp.ones((SUBLANES, A_BAND), F32)
        h_loc = jnp.zeros((SUBLANES, A_BAND), F32)
        for k in range(seg):
            a_k, u_k, _ = step(k)
            h_loc = a_k * h_loc + u_k
            prod = a_k * prod
        c_in = carry_ref[:, cols]
        cum_a, cum_b = _sublane_scan(prod, h_loc)
        seg_end = cum_a * c_in + cum_b
        carry_ref[:, cols] = jnp.broadcast_to(seg_end[SUBLANES - 1:SUBLANES, :], seg_end.shape)
        h_cur = jnp.where(row8 == 0, c_in, pltpu.roll(seg_end, 1, 0))
        for k in range(seg):
            a_k, u_k, sl = step(k)
            h_cur = a_k * h_cur + u_k
            u_ref[sl, cols] = h_cur

        gate = jax.nn.gelu(gate_pre[:, cols])
        ys.append((u_ref[:, cols] * gate).astype(BF16))
        yield

    y = _dot(_segment_permutation(rows, True), jnp.concatenate(ys, axis=1)).astype(BF16)
    return x + _dot(y, w_out_s[...])


def _rglru_layer(x, i, j, g, w_in, conv_w, conv_b, gate_a_w, gate_a_b, gate_x_w, gate_x_b, lam, w_out,
                 mlp_w1, mlp_w2):
    lead = (A_CONV - 1) * SUBLANES
    n_sub = TS_RGLRU // SUB_ROWS
    head_tile = np.zeros((A_BANDS, A_HEAD_DIM, A_BAND), np.float32)
    head_mask = np.zeros((A_BANDS, A_WINDOW, A_BAND), np.float32)
    for band, lo in enumerate(_band_window_starts()):
        col = band * A_BAND + np.arange(A_BAND)
        row = lo + np.arange(A_WINDOW)
        head_tile[band] = col[None, :] % A_HEAD_DIM == np.arange(A_HEAD_DIM)[:, None]
        head_mask[band] = row[:, None] // A_HEAD_DIM == col[None, :] // A_HEAD_DIM
    n_a = w_in.shape[0]
    consts = [
        (_row(g), i), (w_in, j), (conv_w, j), (_row(conv_b), j),
        (gate_a_w.reshape(n_a, D_RNN, A_HEAD_DIM), j), (gate_x_w.reshape(n_a, D_RNN, A_HEAD_DIM), j),
        (jnp.asarray(head_tile, BF16)[None], 0), (jnp.asarray(head_mask, F32)[None], 0),
        (_row(gate_a_b), j), (_row(gate_x_b), j), (_row(lam), j), (w_out, j),
    ]
    scratch = [
        pltpu.VMEM((D_MODEL, 2 * D_RNN), BF16),
        pltpu.VMEM((A_BANDS, A_WINDOW, 2 * A_BAND), BF16),
        pltpu.VMEM((D_RNN, D_MODEL), BF16),
        pltpu.VMEM((n_sub, lead + SUB_ROWS, D_RNN), F32),
        pltpu.VMEM((lead, D_RNN), F32),
        pltpu.VMEM((n_sub, SUB_ROWS, D_RNN), F32),
        pltpu.VMEM((n_sub, SUB_ROWS, D_RNN), F32),
        pltpu.VMEM((n_sub, SUB_ROWS, D_RNN), F32),
        pltpu.VMEM((SUBLANES, D_RNN), F32),
    ]
    return _mixer_call(_rglru_body, "rglru_mixer", TS_RGLRU, x, consts, scratch, mlp_w1, mlp_w2, i)


def _sgu_body(x_ref, g_ref, w_in_ref, ng_ref, w_s_ref, bias_ref, w_out_ref, o_ref,
              w_u_s, w_v_s, w_c_s, w_out_s, mix_ref):
    rows = x_ref.shape[1]

    @pl.when(_is_first_step())
    def _():
        w_u_s[...] = w_in_ref[:, :D_SGU].astype(BF16)
        w_v_s[...] = w_in_ref[:, D_SGU:].astype(BF16)
        w_out_s[...] = w_out_ref[...].astype(BF16)
        t_idx = lax.broadcasted_iota(jnp.int32, (SGU_CHUNK, SGU_CHUNK), 0)
        s_idx = lax.broadcasted_iota(jnp.int32, (SGU_CHUNK, SGU_CHUNK), 1)
        for grp in range(SGU_GROUPS):
            w_c_s[grp] = jnp.where(t_idx >= s_idx, w_s_ref[grp], 0.0).astype(BF16)

    for r0 in range(0, rows, SGU_SUB_ROWS):
        x = x_ref[0, r0:r0 + SGU_SUB_ROWS, :]
        h = _rmsnorm(x, g_ref[...]).astype(BF16)
        v = jax.nn.gelu(_dot(h, w_v_s[...]))
        v = _rmsnorm(v, ng_ref[...]).astype(BF16)
        u = jax.nn.gelu(_dot(h, w_u_s[...]))
        chunk_starts = range(0, SGU_SUB_ROWS, SGU_CHUNK)
        for grp in range(SGU_GROUPS):
            lanes = slice(grp * SGU_GROUP_DIM, (grp + 1) * SGU_GROUP_DIM)
            side_by_side = jnp.concatenate([v[c0:c0 + SGU_CHUNK, lanes] for c0 in chunk_starts], axis=1)
            mixed = _dot(w_c_s[grp], side_by_side)
            for n, c0 in enumerate(chunk_starts):
                mix_ref[r0 + c0:r0 + c0 + SGU_CHUNK, lanes] = (
                    mixed[:, n * SGU_GROUP_DIM:(n + 1) * SGU_GROUP_DIM] + bias_ref[:, lanes])
        y = (u * mix_ref[r0:r0 + SGU_SUB_ROWS, :]).astype(BF16)
        o_ref[0, r0:r0 + SGU_SUB_ROWS, :] = x + _dot(y, w_out_s[...])


def _sgu_layer(x, i, j, g, w_in, norm_g, w_s, s_bias, w_out, mlp_w1, mlp_w2):
    bias = jnp.repeat(jnp.swapaxes(s_bias, 1, 2), SGU_GROUP_DIM, axis=2)
    consts = [(_row(g), i), (w_in, j), (_row(norm_g), j), (w_s, j), (bias, j), (w_out, j)]
    scratch = [
        pltpu.VMEM((D_MODEL, D_SGU), BF16),
        pltpu.VMEM((D_MODEL, D_SGU), BF16),
        pltpu.VMEM((SGU_GROUPS, SGU_CHUNK, SGU_CHUNK), BF16),
        pltpu.VMEM((D_SGU, D_MODEL), BF16),
        pltpu.VMEM((TS_SGU, D_SGU), F32),
    ]
    return _mixer_call(_sgu_body, "sgu_mixer", TS_SGU, x, consts, scratch, mlp_w1, mlp_w2, i)


def _shortconv_body(x_ref, g_ref, w_in_ref, conv_w_ref, w_out_ref, o_ref, w_in_s, w_out_s, hist_ref):
    rows = x_ref.shape[1]

    @pl.when(_is_first_step())
    def _():
        w_in_s[...] = w_in_ref[...].astype(BF16)
        w_out_s[...] = w_out_ref[...].astype(BF16)

    @pl.when(pl.program_id(1) == 0)
    def _():
        hist_ref[0:HIST, :] = jnp.zeros((HIST, D_CONV), F32)

    for r0 in range(0, rows, SUB_ROWS):
        x = x_ref[0, r0:r0 + SUB_ROWS, :]
        h = _rmsnorm(x, g_ref[...]).astype(BF16)
        gb = _dot(h, w_in_s[:, :D_CONV])
        gc = _dot(h, w_in_s[:, D_CONV:2 * D_CONV])
        xv = _dot(h, w_in_s[:, 2 * D_CONV:])
        hist_ref[HIST + r0:HIST + r0 + SUB_ROWS, :] = gc * xv
        conv = None
        for k in range(C_CONV):
            start = HIST + r0 - (C_CONV - 1) + k
            term = conv_w_ref[k:k + 1, :] * hist_ref[start:start + SUB_ROWS, :]
            conv = term if conv is None else conv + term
        y = (gb * conv).astype(BF16)
        o_ref[0, r0:r0 + SUB_ROWS, :] = x + _dot(y, w_out_s[...])
    hist_ref[0:HIST, :] = hist_ref[rows:rows + HIST, :]


def _shortconv_layer(x, i, j, g, w_in, conv_w, w_out, mlp_w1, mlp_w2):
    consts = [(_row(g), i), (w_in, j), (conv_w, j), (w_out, j)]
    scratch = [
        pltpu.VMEM((D_MODEL, 3 * D_CONV), BF16),
        pltpu.VMEM((D_CONV, D_MODEL), BF16),
        pltpu.VMEM((HIST + TS_SHORTCONV, D_CONV), F32),
    ]
    return _mixer_call(_shortconv_body, "shortconv_mixer", TS_SHORTCONV, x, consts, scratch, mlp_w1, mlp_w2, i)


def _mlp_kernel(x_ref, g_ref, w1_ref, w2_ref, fg_ref, o_ref, *, final_norm):
    for r0 in range(0, x_ref.shape[0], MLP_SUB_ROWS):
        x = x_ref[r0:r0 + MLP_SUB_ROWS, :]
        h = _rmsnorm(x, g_ref[...]).astype(BF16)
        acc = x
        for c in range(D_FF // FF_CHUNK):
            cols = slice(c * FF_CHUNK, (c + 1) * FF_CHUNK)
            a = jnp.square(jnp.maximum(_dot(h, w1_ref[:, cols]), 0.0)).astype(BF16)
            acc = acc + _dot(a, w2_ref[cols, :])
        if final_norm:
            acc = _rmsnorm(acc, fg_ref[...])
        o_ref[r0:r0 + MLP_SUB_ROWS, :] = acc


def _mlp_layer(x2d, i, g, w1_bf16, w2_bf16, final_g, final_norm):
    n, _ = x2d.shape
    tm = TM_MLP
    row_spec = pl.BlockSpec((tm, D_MODEL), lambda r: (r, 0))
    return pl.pallas_call(
        functools.partial(_mlp_kernel, final_norm=final_norm),
        grid=(n // tm,),
        in_specs=[
            row_spec,
            _layer_spec(_row(g), i),
            _layer_spec(w1_bf16[None], 0),
            _layer_spec(w2_bf16[None], 0),
            _layer_spec(final_g[None, None, :], 0),
        ],
        out_specs=row_spec,
        out_shape=jax.ShapeDtypeStruct(x2d.shape, F32),
        compiler_params=pltpu.CompilerParams(
            dimension_semantics=("arbitrary",), vmem_limit_bytes=VMEM_LIMIT),
        name="sqrelu_mlp",
    )(x2d, _row(g), w1_bf16[None], w2_bf16[None], final_g[None, None, :])


def kernel(x, norm_mix_g, norm_mlp_g, final_norm_g, a_w_in, a_conv_w, a_conv_b, a_gate_a_w, a_gate_a_b, a_gate_x_w, a_gate_x_b, a_lambda, a_w_out, b_w_in, b_norm_g, b_w_s, b_s_bias, b_w_out, c_w_in, c_conv_w, c_w_out, mlp_w1, mlp_w2):
    b, s, d = x.shape
    assert (b * s) % TM_MLP == 0 and TM_MLP % MLP_SUB_ROWS == 0 and SUB_ROWS % SGU_CHUNK == 0
    for i in range(DEPTH):
        kind, j = i % N_MIXERS, i // N_MIXERS
        if kind == 0:
            x, w1, w2 = _rglru_layer(x, i, j, norm_mix_g, a_w_in, a_conv_w, a_conv_b, a_gate_a_w,
                                     a_gate_a_b, a_gate_x_w, a_gate_x_b, a_lambda, a_w_out,
                                     mlp_w1, mlp_w2)
        elif kind == 1:
            x, w1, w2 = _sgu_layer(x, i, j, norm_mix_g, b_w_in, b_norm_g, b_w_s, b_s_bias, b_w_out,
                                   mlp_w1, mlp_w2)
        else:
            x, w1, w2 = _shortconv_layer(x, i, j, norm_mix_g, c_w_in, c_conv_w, c_w_out, mlp_w1, mlp_w2)
        x = _mlp_layer(x.reshape(b * s, d), i, norm_mlp_g, w1, w2, final_norm_g,
                       final_norm=(i == DEPTH - 1)).reshape(b, s, d)
    return x
```

```python
import functools

import jax
import jax.numpy as jnp
import numpy as np
from jax import lax
from jax.experimental import pallas as pl
from jax.experimental.pallas import tpu as pltpu

D_MODEL = 1024
DEPTH = 4
N_MIXERS = 3
D_RNN = 1280
A_HEADS = 16
A_HEAD_DIM = D_RNN // A_HEADS
A_CONV = 4
LRU_C = 8.0
D_SGU = D_MODEL
SGU_CHUNK = 128
SGU_GROUPS = 8
SGU_GROUP_DIM = D_SGU // SGU_GROUPS
D_CONV = D_MODEL
C_CONV = 3
D_FF = 4 * D_MODEL
EPS = 1e-6
LOG2_E = 1.4426950408889634

SUBLANES = 8
BF16_ROWS = 16
LANES = 128
MXU_COLS = 256
COL_TILE = 2 * MXU_COLS
A_BAND = MXU_COLS
A_BANDS = D_RNN // A_BAND
A_WINDOW = 2 * A_BAND
HIST = SUBLANES
TS_RGLRU = 1024
TS_SGU = 1024
TS_SHORTCONV = 1024
SUB_ROWS = 256
SGU_SUB_ROWS = 512
STREAM_ROWS = 128
TM_MLP = 1024
MLP_SUB_ROWS = 512
FF_CHUNK = 2048
VMEM_LIMIT = 56 * 1024 * 1024

F32 = jnp.float32
BF16 = jnp.bfloat16


def _rmsnorm(x, g):
    return x * lax.rsqrt(jnp.mean(x * x, axis=-1, keepdims=True) + EPS) * g


def _dot(a, b):
    return jnp.dot(a, b, preferred_element_type=F32)


def _layer_spec(stacked, layer):
    zeros = (0,) * (stacked.ndim - 1)
    return pl.BlockSpec((None,) + stacked.shape[1:], lambda *_: (layer,) + zeros,
                        pipeline_mode=pl.Buffered(1))


def _row(stacked):
    return stacked[:, None, :]


def _is_first_step():
    return jnp.logical_and(pl.program_id(0) == 0, pl.program_id(1) == 0)


def _stream_cast(src_hbm, chunk_rows, stage_ref, sem_ref, store):
    n_chunks = src_hbm.shape[0] // chunk_rows
    assert n_chunks * chunk_rows == src_hbm.shape[0]

    def copy(c):
        return pltpu.make_async_copy(src_hbm.at[pl.ds(c * chunk_rows, chunk_rows), :],
                                     stage_ref.at[c % 2], sem_ref.at[c % 2])

    copy(0).start()
    for c in range(n_chunks):
        if c + 1 < n_chunks:
            copy(c + 1).start()
        copy(c).wait()
        store(c * chunk_rows, stage_ref[c % 2])


def _mixer_call(body, name, ts, x, consts, hbm_weights, scratch_shapes, mlp_w1, mlp_w2, mlp_layer):
    b, s, _ = x.shape
    assert s % ts == 0 and ts % SUB_ROWS == 0
    nt = s // ts
    steps = b * nt
    row_spec = pl.BlockSpec((1, ts, D_MODEL), lambda i, j: (i, j, 0))

    def slice_rows(w):
        rows = w.shape[1] // steps
        assert rows * steps == w.shape[1] and rows % BF16_ROWS == 0
        return rows

    def in_slice_spec(w):
        return pl.BlockSpec((None, slice_rows(w), w.shape[2]), lambda i, j: (mlp_layer, i * nt + j, 0))

    def out_slice_spec(w):
        return pl.BlockSpec((slice_rows(w), w.shape[2]), lambda i, j: (i * nt + j, 0))

    n_in = len(consts) + len(hbm_weights)

    def kernel(x_ref, *refs):
        w1_ref, w2_ref, o_ref, w1_out_ref, w2_out_ref = refs[n_in:n_in + 5]
        w1_out_ref[...] = w1_ref[...].astype(BF16)
        w2_out_ref[...] = w2_ref[...].astype(BF16)
        body(x_ref, *refs[:n_in], o_ref, *refs[n_in + 5:])

    return pl.pallas_call(
        kernel,
        grid=(b, nt),
        in_specs=[row_spec] + [_layer_spec(arr, layer) for arr, layer in consts]
        + [pl.BlockSpec(memory_space=pl.ANY)] * len(hbm_weights)
        + [in_slice_spec(mlp_w1), in_slice_spec(mlp_w2)],
        out_specs=[row_spec, out_slice_spec(mlp_w1), out_slice_spec(mlp_w2)],
        out_shape=[jax.ShapeDtypeStruct(x.shape, F32),
                   jax.ShapeDtypeStruct(mlp_w1.shape[1:], BF16),
                   jax.ShapeDtypeStruct(mlp_w2.shape[1:], BF16)],
        scratch_shapes=scratch_shapes,
        compiler_params=pltpu.CompilerParams(
            dimension_semantics=("arbitrary", "arbitrary"), vmem_limit_bytes=VMEM_LIMIT),
        name=name,
    )(x, *[arr for arr, _ in consts], *hbm_weights, mlp_w1, mlp_w2)


def _sublane_scan(a, b):
    row = lax.broadcasted_iota(jnp.int32, a.shape, 0)
    for d in (1, 2, 4):
        keep = row >= d
        a_prev = jnp.where(keep, pltpu.roll(a, d, 0), 1.0)
        b_prev = jnp.where(keep, pltpu.roll(b, d, 0), 0.0)
        b = a * b_prev + b
        a = a * a_prev
    return a, b


def _segment_permutation(rows, inverse):
    seg = rows // SUBLANES
    i0 = lax.broadcasted_iota(jnp.int32, (rows, rows), 0)
    i1 = lax.broadcasted_iota(jnp.int32, (rows, rows), 1)
    major, natural = (i1, i0) if inverse else (i0, i1)
    hit = natural == (major % SUBLANES) * seg + major // SUBLANES
    return jnp.where(hit, 1.0, 0.0).astype(BF16)


def _band_window_starts():
    starts = []
    for band in range(A_BANDS):
        first_head = (band * A_BAND) // A_HEAD_DIM
        last_head = -(-((band + 1) * A_BAND) // A_HEAD_DIM)
        lo = min((first_head * A_HEAD_DIM) // LANES * LANES, D_RNN - A_WINDOW)
        assert lo <= first_head * A_HEAD_DIM and last_head * A_HEAD_DIM <= lo + A_WINDOW
        starts.append(lo)
    return starts


def _rglru_body(x_ref, g_ref, conv_w_ref, conv_b_ref, gate_a_ref, gate_x_ref,
                head_tile_ref, head_mask_ref, b_a_ref, b_x_ref, lam_ref, w_in_hbm, w_out_hbm, o_ref,
                w_in_s, w_band_s, w_out_s, ext_ref, tail_ref, xr_ref, a_ref, u_ref, carry_ref,
                stage_in, stage_out, sem, *, layer):
    window_starts = _band_window_starts()

    @pl.when(_is_first_step())
    def _():
        def store_in(r0, chunk):
            w_in_s[r0:r0 + STREAM_ROWS, :D_RNN] = chunk[:, D_RNN:].astype(BF16)
            w_in_s[r0:r0 + STREAM_ROWS, D_RNN:] = chunk[:, :D_RNN].astype(BF16)

        def store_out(r0, chunk):
            w_out_s[r0:r0 + STREAM_ROWS, :] = chunk.astype(BF16)

        _stream_cast(w_in_hbm.at[layer], STREAM_ROWS, stage_in, sem, store_in)
        _stream_cast(w_out_hbm.at[layer], STREAM_ROWS, stage_out, sem, store_out)
        for band, lo in enumerate(window_starts):
            for part, gate_ref in enumerate((gate_a_ref, gate_x_ref)):
                tiled = _dot(gate_ref[lo:lo + A_WINDOW, :].astype(BF16), head_tile_ref[band])
                w_band_s[band, :, part * A_BAND:(part + 1) * A_BAND] = (
                    tiled * head_mask_ref[band]).astype(BF16)

    @pl.when(pl.program_id(1) == 0)
    def _():
        tail_ref[...] = jnp.zeros(tail_ref.shape, F32)
        carry_ref[...] = jnp.zeros(carry_ref.shape, F32)

    n_sub = x_ref.shape[1] // SUB_ROWS

    def front(sub):
        x = x_ref[0, sub * SUB_ROWS:(sub + 1) * SUB_ROWS, :]
        return _rglru_front(x, g_ref, conv_w_ref, conv_b_ref, w_in_s,
                            ext_ref.at[sub], tail_ref, xr_ref.at[sub])

    state, _ = _interleave(front(0), None)
    for sub in range(n_sub):
        back = _rglru_back(*state, b_a_ref, b_x_ref, lam_ref, w_band_s, w_out_s,
                           xr_ref.at[sub], a_ref.at[sub], u_ref.at[sub], carry_ref)
        out, state = _interleave(back, front(sub + 1) if sub + 1 < n_sub else None)
        o_ref[0, sub * SUB_ROWS:(sub + 1) * SUB_ROWS, :] = out


def _interleave(first, second):
    results = [None, None]
    live = {0: first, 1: second}
    live = {k: g for k, g in live.items() if g is not None}
    while live:
        for k in list(live):
            try:
                next(live[k])
            except StopIteration as done:
                results[k] = done.value
                del live[k]
    return results


def _rglru_front(x, g_ref, conv_w_ref, conv_b_ref, w_in_s, ext_ref, tail_ref, xr_ref):
    rows = SUB_ROWS
    lead = (A_CONV - 1) * SUBLANES
    h = _rmsnorm(x, g_ref[...]).astype(BF16)
    hp = _dot(_segment_permutation(rows, False), h).astype(BF16)
    chunks = []
    for c0 in range(0, 2 * D_RNN, COL_TILE):
        chunks.append(_dot(hp, w_in_s[:, c0:c0 + COL_TILE]))
        yield
    proj = jnp.concatenate(chunks, axis=1)
    ext_ref[lead:lead + rows, :] = proj[:, :D_RNN]
    row8 = lax.broadcasted_iota(jnp.int32, (SUBLANES, D_RNN), 0)
    for j in range(A_CONV - 1):
        sl = slice(j * SUBLANES, (j + 1) * SUBLANES)
        cur = ext_ref[rows + j * SUBLANES:rows + (j + 1) * SUBLANES, :]
        ext_ref[sl, :] = jnp.where(row8 == 0, pltpu.roll(tail_ref[sl, :], 1, 0), pltpu.roll(cur, 1, 0))
        tail_ref[sl, :] = cur
    xr_all = conv_b_ref[...]
    for k in range(A_CONV):
        xr_all = xr_all + conv_w_ref[k:k + 1, :] * ext_ref[k * SUBLANES:k * SUBLANES + rows, :]
    xr_ref[...] = xr_all
    return x, xr_all.astype(BF16), proj[:, D_RNN:]


def _rglru_back(x, xr_bf16, gate_pre, b_a_ref, b_x_ref, lam_ref, w_band_s, w_out_s,
                xr_ref, a_ref, u_ref, carry_ref):
    rows = SUB_ROWS
    seg = rows // SUBLANES
    window_starts = _band_window_starts()
    row8 = lax.broadcasted_iota(jnp.int32, (SUBLANES, A_BAND), 0)

    ys = []
    for band, lo in enumerate(window_starts):
        cols = slice(band * A_BAND, (band + 1) * A_BAND)
        xr = xr_ref[:, cols]
        ra = _dot(xr_bf16[:, lo:lo + A_WINDOW], w_band_s[band])
        r = jax.nn.sigmoid(ra[:, :A_BAND] + b_a_ref[:, cols])
        ig = jax.nn.sigmoid(ra[:, A_BAND:] + b_x_ref[:, cols])
        lam = lam_ref[:, cols]
        softplus_neg_lam = jnp.maximum(-lam, 0.0) + jnp.log1p(jnp.exp(-jnp.abs(lam)))
        a = jnp.exp2(r * ((-LRU_C * LOG2_E) * softplus_neg_lam))
        a_ref[:, cols] = a
        z = 1.0 - a * a
        u_ref[:, cols] = jnp.where(z > 0.0, z * lax.rsqrt(z), 0.0) * (ig * xr)

        def step(k):
            sl = slice(k * SUBLANES, (k + 1) * SUBLANES)
            return a_ref[sl, cols], u_ref[sl, cols], sl

        prod = jnp.ones((SUBLANES, A_BAND), F32)
        h_loc = jnp.zeros((SUBLANES, A_BAND), F32)
        for k in range(seg):
            a_k, u_k, _ = step(k)
            h_loc = a_k * h_loc + u_k
            prod = a_k * prod
        c_in = carry_ref[:, cols]
        cum_a, cum_b = _sublane_scan(prod, h_loc)
        seg_end = cum_a * c_in + cum_b
        carry_ref[:, cols] = jnp.broadcast_to(seg_end[SUBLANES - 1:SUBLANES, :], seg_end.shape)
        h_cur = jnp.where(row8 == 0, c_in, pltpu.roll(seg_end, 1, 0))
        for k in range(seg):
            a_k, u_k, sl = step(k)
            h_cur = a_k * h_cur + u_k
            u_ref[sl, cols] = h_cur

        gate = jax.nn.gelu(gate_pre[:, cols])
        ys.append((u_ref[:, cols] * gate).astype(BF16))
        yield

    y = _dot(_segment_permutation(rows, True), jnp.concatenate(ys, axis=1)).astype(BF16)
    return x + _dot(y, w_out_s[...])


def _rglru_layer(x, i, j, g, w_in, conv_w, conv_b, gate_a_w, gate_a_b, gate_x_w, gate_x_b, lam, w_out,
                 mlp_w1, mlp_w2):
    lead = (A_CONV - 1) * SUBLANES
    n_sub = TS_RGLRU // SUB_ROWS
    head_tile = np.zeros((A_BANDS, A_HEAD_DIM, A_BAND), np.float32)
    head_mask = np.zeros((A_BANDS, A_WINDOW, A_BAND), np.float32)
    for band, lo in enumerate(_band_window_starts()):
        col = band * A_BAND + np.arange(A_BAND)
        row = lo + np.arange(A_WINDOW)
        head_tile[band] = col[None, :] % A_HEAD_DIM == np.arange(A_HEAD_DIM)[:, None]
        head_mask[band] = row[:, None] // A_HEAD_DIM == col[None, :] // A_HEAD_DIM
    n_a = w_in.shape[0]
    consts = [
        (_row(g), i), (conv_w, j), (_row(conv_b), j),
        (gate_a_w.reshape(n_a, D_RNN, A_HEAD_DIM), j), (gate_x_w.reshape(n_a, D_RNN, A_HEAD_DIM), j),
        (jnp.asarray(head_tile, BF16)[None], 0), (jnp.asarray(head_mask, BF16)[None], 0),
        (_row(gate_a_b), j), (_row(gate_x_b), j), (_row(lam), j),
    ]
    scratch = [
        pltpu.VMEM((D_MODEL, 2 * D_RNN), BF16),
        pltpu.VMEM((A_BANDS, A_WINDOW, 2 * A_BAND), BF16),
        pltpu.VMEM((D_RNN, D_MODEL), BF16),
        pltpu.VMEM((n_sub, lead + SUB_ROWS, D_RNN), F32),
        pltpu.VMEM((lead, D_RNN), F32),
        pltpu.VMEM((n_sub, SUB_ROWS, D_RNN), F32),
        pltpu.VMEM((n_sub, SUB_ROWS, D_RNN), F32),
        pltpu.VMEM((n_sub, SUB_ROWS, D_RNN), F32),
        pltpu.VMEM((SUBLANES, D_RNN), F32),
        pltpu.VMEM((2, STREAM_ROWS, 2 * D_RNN), F32),
        pltpu.VMEM((2, STREAM_ROWS, D_MODEL), F32),
        pltpu.SemaphoreType.DMA((2,)),
    ]
    return _mixer_call(functools.partial(_rglru_body, layer=j), "rglru_mixer", TS_RGLRU, x, consts,
                       [w_in, w_out], scratch, mlp_w1, mlp_w2, i)


def _sgu_body(x_ref, g_ref, ng_ref, w_s_ref, bias_ref, w_in_hbm, w_out_hbm, o_ref,
              w_u_s, w_v_s, w_c_s, w_out_s, mix_ref, stage_in, stage_out, sem, *, layer):
    rows = x_ref.shape[1]

    @pl.when(_is_first_step())
    def _():
        def store_in(r0, chunk):
            w_u_s[r0:r0 + STREAM_ROWS, :] = chunk[:, :D_SGU].astype(BF16)
            w_v_s[r0:r0 + STREAM_ROWS, :] = chunk[:, D_SGU:].astype(BF16)

        def store_out(r0, chunk):
            w_out_s[r0:r0 + STREAM_ROWS, :] = chunk.astype(BF16)

        _stream_cast(w_in_hbm.at[layer], STREAM_ROWS, stage_in, sem, store_in)
        _stream_cast(w_out_hbm.at[layer], STREAM_ROWS, stage_out, sem, store_out)
        t_idx = lax.broadcasted_iota(jnp.int32, (SGU_CHUNK, SGU_CHUNK), 0)
        s_idx = lax.broadcasted_iota(jnp.int32, (SGU_CHUNK, SGU_CHUNK), 1)
        for grp in range(SGU_GROUPS):
            w_c_s[grp] = jnp.where(t_idx >= s_idx, w_s_ref[grp], 0.0).astype(BF16)

    for r0 in range(0, rows, SGU_SUB_ROWS):
        x = x_ref[0, r0:r0 + SGU_SUB_ROWS, :]
        h = _rmsnorm(x, g_ref[...]).astype(BF16)
        v = jax.nn.gelu(_dot(h, w_v_s[...]))
        v = _rmsnorm(v, ng_ref[...]).astype(BF16)
        u = jax.nn.gelu(_dot(h, w_u_s[...]))
        chunk_starts = range(0, SGU_SUB_ROWS, SGU_CHUNK)
        for grp in range(SGU_GROUPS):
            lanes = slice(grp * SGU_GROUP_DIM, (grp + 1) * SGU_GROUP_DIM)
            side_by_side = jnp.concatenate([v[c0:c0 + SGU_CHUNK, lanes] for c0 in chunk_starts], axis=1)
            mixed = _dot(w_c_s[grp], side_by_side)
            for n, c0 in enumerate(chunk_starts):
                mix_ref[r0 + c0:r0 + c0 + SGU_CHUNK, lanes] = (
                    mixed[:, n * SGU_GROUP_DIM:(n + 1) * SGU_GROUP_DIM] + bias_ref[:, lanes])
        y = (u * mix_ref[r0:r0 + SGU_SUB_ROWS, :]).astype(BF16)
        o_ref[0, r0:r0 + SGU_SUB_ROWS, :] = x + _dot(y, w_out_s[...])


def _sgu_layer(x, i, j, g, w_in, norm_g, w_s, s_bias, w_out, mlp_w1, mlp_w2):
    bias = jnp.repeat(jnp.swapaxes(s_bias, 1, 2), SGU_GROUP_DIM, axis=2)
    consts = [(_row(g), i), (_row(norm_g), j), (w_s, j), (bias, j)]
    scratch = [
        pltpu.VMEM((D_MODEL, D_SGU), BF16),
        pltpu.VMEM((D_MODEL, D_SGU), BF16),
        pltpu.VMEM((SGU_GROUPS, SGU_CHUNK, SGU_CHUNK), BF16),
        pltpu.VMEM((D_SGU, D_MODEL), BF16),
        pltpu.VMEM((TS_SGU, D_SGU), F32),
        pltpu.VMEM((2, STREAM_ROWS, 2 * D_SGU), F32),
        pltpu.VMEM((2, STREAM_ROWS, D_MODEL), F32),
        pltpu.SemaphoreType.DMA((2,)),
    ]
    return _mixer_call(functools.partial(_sgu_body, layer=j), "sgu_mixer", TS_SGU, x, consts,
                       [w_in, w_out], scratch, mlp_w1, mlp_w2, i)


def _shortconv_body(x_ref, g_ref, conv_w_ref, w_in_hbm, w_out_hbm, o_ref, w_in_s, w_out_s, hist_ref,
                    stage_in, stage_out, sem, *, layer):
    rows = x_ref.shape[1]

    @pl.when(_is_first_step())
    def _():
        def store_in(r0, chunk):
            w_in_s[r0:r0 + STREAM_ROWS, :] = chunk.astype(BF16)

        def store_out(r0, chunk):
            w_out_s[r0:r0 + STREAM_ROWS, :] = chunk.astype(BF16)

        _stream_cast(w_in_hbm.at[layer], STREAM_ROWS, stage_in, sem, store_in)
        _stream_cast(w_out_hbm.at[layer], STREAM_ROWS, stage_out, sem, store_out)

    @pl.when(pl.program_id(1) == 0)
    def _():
        hist_ref[0:HIST, :] = jnp.zeros((HIST, D_CONV), F32)

    for r0 in range(0, rows, SUB_ROWS):
        x = x_ref[0, r0:r0 + SUB_ROWS, :]
        h = _rmsnorm(x, g_ref[...]).astype(BF16)
        gb = _dot(h, w_in_s[:, :D_CONV])
        gc = _dot(h, w_in_s[:, D_CONV:2 * D_CONV])
        xv = _dot(h, w_in_s[:, 2 * D_CONV:])
        hist_ref[HIST + r0:HIST + r0 + SUB_ROWS, :] = gc * xv
        conv = None
        for k in range(C_CONV):
            start = HIST + r0 - (C_CONV - 1) + k
            term = conv_w_ref[k:k + 1, :] * hist_ref[start:start + SUB_ROWS, :]
            conv = term if conv is None else conv + term
        y = (gb * conv).astype(BF16)
        o_ref[0, r0:r0 + SUB_ROWS, :] = x + _dot(y, w_out_s[...])
    hist_ref[0:HIST, :] = hist_ref[rows:rows + HIST, :]


def _shortconv_layer(x, i, j, g, w_in, conv_w, w_out, mlp_w1, mlp_w2):
    consts = [(_row(g), i), (conv_w, j)]
    scratch = [
        pltpu.VMEM((D_MODEL, 3 * D_CONV), BF16),
        pltpu.VMEM((D_CONV, D_MODEL), BF16),
        pltpu.VMEM((HIST + TS_SHORTCONV, D_CONV), F32),
        pltpu.VMEM((2, STREAM_ROWS, 3 * D_CONV), F32),
        pltpu.VMEM((2, STREAM_ROWS, D_MODEL), F32),
        pltpu.SemaphoreType.DMA((2,)),
    ]
    return _mixer_call(functools.partial(_shortconv_body, layer=j), "shortconv_mixer", TS_SHORTCONV, x,
                       consts, [w_in, w_out], scratch, mlp_w1, mlp_w2, i)


def _mlp_kernel(x_ref, g_ref, w1_ref, w2_ref, fg_ref, o_ref, *, final_norm):
    for r0 in range(0, x_ref.shape[0], MLP_SUB_ROWS):
        x = x_ref[r0:r0 + MLP_SUB_ROWS, :]
        h = _rmsnorm(x, g_ref[...]).astype(BF16)
        acc = x
        for c in range(D_FF // FF_CHUNK):
            cols = slice(c * FF_CHUNK, (c + 1) * FF_CHUNK)
            a = jnp.square(jnp.maximum(_dot(h, w1_ref[:, cols]), 0.0)).astype(BF16)
            acc = acc + _dot(a, w2_ref[cols, :])
        if final_norm:
            acc = _rmsnorm(acc, fg_ref[...])
        o_ref[r0:r0 + MLP_SUB_ROWS, :] = acc


def _mlp_layer(x2d, i, g, w1_bf16, w2_bf16, final_g, final_norm):
    n, _ = x2d.shape
    tm = TM_MLP
    row_spec = pl.BlockSpec((tm, D_MODEL), lambda r: (r, 0))
    return pl.pallas_call(
        functools.partial(_mlp_kernel, final_norm=final_norm),
        grid=(n // tm,),
        in_specs=[
            row_spec,
            _layer_spec(_row(g), i),
            _layer_spec(w1_bf16[None], 0),
            _layer_spec(w2_bf16[None], 0),
            _layer_spec(final_g[None, None, :], 0),
        ],
        out_specs=row_spec,
        out_shape=jax.ShapeDtypeStruct(x2d.shape, F32),
        compiler_params=pltpu.CompilerParams(
            dimension_semantics=("arbitrary",), vmem_limit_bytes=VMEM_LIMIT),
        name="sqrelu_mlp",
    )(x2d, _row(g), w1_bf16[None], w2_bf16[None], final_g[None, None, :])


def kernel(x, norm_mix_g, norm_mlp_g, final_norm_g, a_w_in, a_conv_w, a_conv_b, a_gate_a_w, a_gate_a_b, a_gate_x_w, a_gate_x_b, a_lambda, a_w_out, b_w_in, b_norm_g, b_w_s, b_s_bias, b_w_out, c_w_in, c_conv_w, c_w_out, mlp_w1, mlp_w2):
    b, s, d = x.shape
    assert (b * s) % TM_MLP == 0 and TM_MLP % MLP_SUB_ROWS == 0 and SUB_ROWS % SGU_CHUNK == 0
    for i in range(DEPTH):
        kind, j = i % N_MIXERS, i // N_MIXERS
        if kind == 0:
            x, w1, w2 = _rglru_layer(x, i, j, norm_mix_g, a_w_in, a_conv_w, a_conv_b, a_gate_a_w,
                                     a_gate_a_b, a_gate_x_w, a_gate_x_b, a_lambda, a_w_out,
                                     mlp_w1, mlp_w2)
        elif kind == 1:
            x, w1, w2 = _sgu_layer(x, i, j, norm_mix_g, b_w_in, b_norm_g, b_w_s, b_s_bias, b_w_out,
                                   mlp_w1, mlp_w2)
        else:
            x, w1, w2 = _shortconv_layer(x, i, j, norm_mix_g, c_w_in, c_conv_w, c_w_out, mlp_w1, mlp_w2)
        x = _mlp_layer(x.reshape(b * s, d), i, norm_mlp_g, w1, w2, final_norm_g,
                       final_norm=(i == DEPTH - 1)).reshape(b, s, d)
    return x
```

```python
import functools

import jax
import jax.numpy as jnp
import numpy as np
from jax import lax
from jax.experimental import pallas as pl
from jax.experimental.pallas import tpu as pltpu

D_MODEL = 1024
DEPTH = 4
N_MIXERS = 3
D_RNN = 1280
A_HEADS = 16
A_HEAD_DIM = D_RNN // A_HEADS
A_CONV = 4
LRU_C = 8.0
D_SGU = D_MODEL
SGU_CHUNK = 128
SGU_GROUPS = 8
SGU_GROUP_DIM = D_SGU // SGU_GROUPS
D_CONV = D_MODEL
C_CONV = 3
D_FF = 4 * D_MODEL
EPS = 1e-6
LOG2_E = 1.4426950408889634

SUBLANES = 8
BF16_ROWS = 16
LANES = 128
MXU_COLS = 256
COL_TILE = 2 * MXU_COLS
A_BAND = MXU_COLS
A_BANDS = D_RNN // A_BAND
A_WINDOW = 2 * A_BAND
HIST = SUBLANES
TS_RGLRU = 512
TS_SGU = 1024
TS_SHORTCONV = 1024
SUB_ROWS = 256
SGU_SUB_ROWS = 512
TM_MLP = 2048
MLP_SUB_ROWS = 512
FF_CHUNK = 2048
VMEM_LIMIT = 56 * 1024 * 1024

F32 = jnp.float32
BF16 = jnp.bfloat16


def _rmsnorm(x, g):
    return x * lax.rsqrt(jnp.mean(x * x, axis=-1, keepdims=True) + EPS) * g


def _dot(a, b):
    return jnp.dot(a, b, preferred_element_type=F32)


def _gelu_tanh(x):
    c = 0.7978845608028654
    half_x = 0.5 * x
    return half_x + half_x * jnp.tanh(x * (c + (c * 0.044715) * (x * x)))


def _layer_spec(stacked, layer):
    zeros = (0,) * (stacked.ndim - 1)
    return pl.BlockSpec((None,) + stacked.shape[1:], lambda *_: (layer,) + zeros,
                        pipeline_mode=pl.Buffered(1))


def _row(stacked):
    return stacked[:, None, :]


def _is_first_step():
    return jnp.logical_and(pl.program_id(0) == 0, pl.program_id(1) == 0)


def _mixer_call(body, name, ts, x, consts, scratch_shapes, mlp_w1, mlp_w2, mlp_layer):
    b, s, _ = x.shape
    assert s % ts == 0 and ts % SUB_ROWS == 0
    nt = s // ts
    steps = b * nt
    row_spec = pl.BlockSpec((1, ts, D_MODEL), lambda i, j: (i, j, 0))

    def slice_rows(w):
        rows = w.shape[1] // steps
        assert rows * steps == w.shape[1] and rows % BF16_ROWS == 0
        return rows

    def in_slice_spec(w):
        return pl.BlockSpec((None, slice_rows(w), w.shape[2]), lambda i, j: (mlp_layer, i * nt + j, 0))

    def out_slice_spec(w):
        return pl.BlockSpec((slice_rows(w), w.shape[2]), lambda i, j: (i * nt + j, 0))

    n_const = len(consts)

    def kernel(x_ref, *refs):
        const_refs = refs[:n_const]
        w1_ref, w2_ref, o_ref, w1_out_ref, w2_out_ref = refs[n_const:n_const + 5]
        w1_out_ref[...] = w1_ref[...].astype(BF16)
        w2_out_ref[...] = w2_ref[...].astype(BF16)
        body(x_ref, *const_refs, o_ref, *refs[n_const + 5:])

    return pl.pallas_call(
        kernel,
        grid=(b, nt),
        in_specs=[row_spec] + [_layer_spec(arr, layer) for arr, layer in consts]
        + [in_slice_spec(mlp_w1), in_slice_spec(mlp_w2)],
        out_specs=[row_spec, out_slice_spec(mlp_w1), out_slice_spec(mlp_w2)],
        out_shape=[jax.ShapeDtypeStruct(x.shape, F32),
                   jax.ShapeDtypeStruct(mlp_w1.shape[1:], BF16),
                   jax.ShapeDtypeStruct(mlp_w2.shape[1:], BF16)],
        scratch_shapes=scratch_shapes,
        compiler_params=pltpu.CompilerParams(
            dimension_semantics=("arbitrary", "arbitrary"), vmem_limit_bytes=VMEM_LIMIT),
        name=name,
    )(x, *[arr for arr, _ in consts], mlp_w1, mlp_w2)


def _sublane_scan(a, b):
    row = lax.broadcasted_iota(jnp.int32, a.shape, 0)
    for d in (1, 2, 4):
        keep = row >= d
        a_prev = jnp.where(keep, pltpu.roll(a, d, 0), 1.0)
        b_prev = jnp.where(keep, pltpu.roll(b, d, 0), 0.0)
        b = a * b_prev + b
        a = a * a_prev
    return a, b


def _segment_permutation(rows, inverse):
    seg = rows // SUBLANES
    i0 = lax.broadcasted_iota(jnp.int32, (rows, rows), 0)
    i1 = lax.broadcasted_iota(jnp.int32, (rows, rows), 1)
    major, natural = (i1, i0) if inverse else (i0, i1)
    hit = natural == (major % SUBLANES) * seg + major // SUBLANES
    return jnp.where(hit, 1.0, 0.0).astype(BF16)


def _band_window_starts():
    starts = []
    for band in range(A_BANDS):
        first_head = (band * A_BAND) // A_HEAD_DIM
        last_head = -(-((band + 1) * A_BAND) // A_HEAD_DIM)
        lo = min((first_head * A_HEAD_DIM) // LANES * LANES, D_RNN - A_WINDOW)
        assert lo <= first_head * A_HEAD_DIM and last_head * A_HEAD_DIM <= lo + A_WINDOW
        starts.append(lo)
    return starts


def _rglru_body(x_ref, g_ref, w_in_ref, conv_w_ref, conv_b_ref, gate_a_ref, gate_x_ref,
                head_tile_ref, head_mask_ref, b_a_ref, b_x_ref, lam_ref, w_out_ref, o_ref,
                w_in_s, w_band_s, w_out_s, hp_ref, ext_ref, tail_ref, xr_ref, xrb_ref, gate_ref,
                a_ref, u_ref, carry_ref):
    window_starts = _band_window_starts()
    n_sub = x_ref.shape[1] // SUB_ROWS

    @pl.when(_is_first_step())
    def _():
        w_in_s[:, :D_RNN] = w_in_ref[:, D_RNN:].astype(BF16)
        w_in_s[:, D_RNN:] = w_in_ref[:, :D_RNN].astype(BF16)
        w_out_s[...] = w_out_ref[...].astype(BF16)
        for band, lo in enumerate(window_starts):
            for part, gate_ref in enumerate((gate_a_ref, gate_x_ref)):
                tiled = _dot(gate_ref[lo:lo + A_WINDOW, :].astype(BF16), head_tile_ref[band])
                w_band_s[band, :, part * A_BAND:(part + 1) * A_BAND] = (
                    tiled * head_mask_ref[band]).astype(BF16)

    @pl.when(pl.program_id(1) == 0)
    def _():
        tail_ref[...] = jnp.zeros(tail_ref.shape, F32)
        carry_ref[...] = jnp.zeros(carry_ref.shape, F32)

    def x_view(sub):
        return x_ref.at[0, pl.ds(sub * SUB_ROWS, SUB_ROWS), :]

    def front(sub):
        return _rglru_front(x_view(sub), g_ref, conv_w_ref, conv_b_ref, w_in_s, hp_ref.at[sub],
                            ext_ref.at[sub], tail_ref, xr_ref.at[sub], xrb_ref.at[sub], gate_ref.at[sub])

    _interleave(front(0), None)
    for sub in range(n_sub):
        back = _rglru_back(x_view(sub), xr_ref.at[sub], xrb_ref.at[sub], gate_ref.at[sub],
                           b_a_ref, b_x_ref, lam_ref, w_band_s, w_out_s,
                           a_ref.at[sub], u_ref.at[sub], carry_ref)
        out, _ = _interleave(back, front(sub + 1) if sub + 1 < n_sub else None)
        o_ref[0, sub * SUB_ROWS:(sub + 1) * SUB_ROWS, :] = out


def _interleave(first, second):
    results = [None, None]
    live = {k: g for k, g in enumerate((first, second)) if g is not None}
    while live:
        for k in list(live):
            try:
                next(live[k])
            except StopIteration as done:
                results[k] = done.value
                del live[k]
    return results


def _rglru_front(x_view, g_ref, conv_w_ref, conv_b_ref, w_in_s, hp_ref, ext_ref, tail_ref,
                 xr_ref, xrb_ref, gate_ref):
    rows = SUB_ROWS
    lead = (A_CONV - 1) * SUBLANES
    h = _rmsnorm(x_view[...], g_ref[...]).astype(BF16)
    hp_ref[...] = _dot(_segment_permutation(rows, False), h).astype(BF16)
    for c0 in range(0, 2 * D_RNN, COL_TILE):
        chunk = _dot(hp_ref[...], w_in_s[:, c0:c0 + COL_TILE])
        n_rec = min(max(D_RNN - c0, 0), COL_TILE)
        if n_rec:
            ext_ref[lead:lead + rows, c0:c0 + n_rec] = chunk[:, :n_rec]
        if n_rec < COL_TILE:
            g0 = c0 + n_rec - D_RNN
            gate_ref[:, g0:g0 + COL_TILE - n_rec] = chunk[:, n_rec:]
        yield
    row8 = lax.broadcasted_iota(jnp.int32, (SUBLANES, D_RNN), 0)
    for j in range(A_CONV - 1):
        sl = slice(j * SUBLANES, (j + 1) * SUBLANES)
        cur = ext_ref[rows + j * SUBLANES:rows + (j + 1) * SUBLANES, :]
        ext_ref[sl, :] = jnp.where(row8 == 0, pltpu.roll(tail_ref[sl, :], 1, 0), pltpu.roll(cur, 1, 0))
        tail_ref[sl, :] = cur
    xr_all = conv_b_ref[...]
    for k in range(A_CONV):
        xr_all = xr_all + conv_w_ref[k:k + 1, :] * ext_ref[k * SUBLANES:k * SUBLANES + rows, :]
    xr_ref[...] = xr_all
    xrb_ref[...] = xr_all.astype(BF16)


def _rglru_back(x_view, xr_ref, xrb_ref, gate_ref, b_a_ref, b_x_ref, lam_ref, w_band_s, w_out_s,
                a_ref, u_ref, carry_ref):
    rows = SUB_ROWS
    seg = rows // SUBLANES
    window_starts = _band_window_starts()
    row8 = lax.broadcasted_iota(jnp.int32, (SUBLANES, A_BAND), 0)

    ys = []
    for band, lo in enumerate(window_starts):
        cols = slice(band * A_BAND, (band + 1) * A_BAND)
        xr = xr_ref[:, cols]
        ra = _dot(xrb_ref[:, lo:lo + A_WINDOW], w_band_s[band])
        r = jax.nn.sigmoid(ra[:, :A_BAND] + b_a_ref[:, cols])
        ig = jax.nn.sigmoid(ra[:, A_BAND:] + b_x_ref[:, cols])
        lam = lam_ref[:, cols]
        softplus_neg_lam = jnp.maximum(-lam, 0.0) + jnp.log1p(jnp.exp(-jnp.abs(lam)))
        a = jnp.exp2(r * ((-LRU_C * LOG2_E) * softplus_neg_lam))
        a_ref[:, cols] = a
        z = 1.0 - a * a
        u_ref[:, cols] = jnp.where(z > 0.0, z * lax.rsqrt(z), 0.0) * (ig * xr)

        def step(k):
            sl = slice(k * SUBLANES, (k + 1) * SUBLANES)
            return a_ref[sl, cols], u_ref[sl, cols], sl

        prod = jnp.ones((SUBLANES, A_BAND), F32)
        h_loc = jnp.zeros((SUBLANES, A_BAND), F32)
        for k in range(seg):
            a_k, u_k, _ = step(k)
            h_loc = a_k * h_loc + u_k
            prod = a_k * prod
        c_in = carry_ref[:, cols]
        cum_a, cum_b = _sublane_scan(prod, h_loc)
        seg_end = cum_a * c_in + cum_b
        carry_ref[:, cols] = jnp.broadcast_to(seg_end[SUBLANES - 1:SUBLANES, :], seg_end.shape)
        h_cur = jnp.where(row8 == 0, c_in, pltpu.roll(seg_end, 1, 0))
        for k in range(seg):
            a_k, u_k, sl = step(k)
            h_cur = a_k * h_cur + u_k
            u_ref[sl, cols] = h_cur

        gate = _gelu_tanh(gate_ref[:, cols])
        ys.append((u_ref[:, cols] * gate).astype(BF16))
        yield

    y = _dot(_segment_permutation(rows, True), jnp.concatenate(ys, axis=1)).astype(BF16)
    return x_view[...] + _dot(y, w_out_s[...])


def _rglru_layer(x, i, j, g, w_in, conv_w, conv_b, gate_a_w, gate_a_b, gate_x_w, gate_x_b, lam, w_out,
                 mlp_w1, mlp_w2):
    lead = (A_CONV - 1) * SUBLANES
    n_sub = TS_RGLRU // SUB_ROWS
    head_tile = np.zeros((A_BANDS, A_HEAD_DIM, A_BAND), np.float32)
    head_mask = np.zeros((A_BANDS, A_WINDOW, A_BAND), np.float32)
    for band, lo in enumerate(_band_window_starts()):
        col = band * A_BAND + np.arange(A_BAND)
        row = lo + np.arange(A_WINDOW)
        head_tile[band] = col[None, :] % A_HEAD_DIM == np.arange(A_HEAD_DIM)[:, None]
        head_mask[band] = row[:, None] // A_HEAD_DIM == col[None, :] // A_HEAD_DIM
    n_a = w_in.shape[0]
    consts = [
        (_row(g), i), (w_in, j), (conv_w, j), (_row(conv_b), j),
        (gate_a_w.reshape(n_a, D_RNN, A_HEAD_DIM), j), (gate_x_w.reshape(n_a, D_RNN, A_HEAD_DIM), j),
        (jnp.asarray(head_tile, BF16)[None], 0), (jnp.asarray(head_mask, BF16)[None], 0),
        (_row(gate_a_b), j), (_row(gate_x_b), j), (_row(lam), j), (w_out, j),
    ]
    scratch = [
        pltpu.VMEM((D_MODEL, 2 * D_RNN), BF16),
        pltpu.VMEM((A_BANDS, A_WINDOW, 2 * A_BAND), BF16),
        pltpu.VMEM((D_RNN, D_MODEL), BF16),
        pltpu.VMEM((n_sub, SUB_ROWS, D_MODEL), BF16),
        pltpu.VMEM((n_sub, lead + SUB_ROWS, D_RNN), F32),
        pltpu.VMEM((lead, D_RNN), F32),
        pltpu.VMEM((n_sub, SUB_ROWS, D_RNN), F32),
        pltpu.VMEM((n_sub, SUB_ROWS, D_RNN), BF16),
        pltpu.VMEM((n_sub, SUB_ROWS, D_RNN), F32),
        pltpu.VMEM((n_sub, SUB_ROWS, D_RNN), F32),
        pltpu.VMEM((n_sub, SUB_ROWS, D_RNN), F32),
        pltpu.VMEM((SUBLANES, D_RNN), F32),
    ]
    return _mixer_call(_rglru_body, "rglru_mixer", TS_RGLRU, x, consts, scratch, mlp_w1, mlp_w2, i)


def _sgu_body(x_ref, g_ref, w_in_ref, ng_ref, w_s_ref, bias_ref, w_out_ref, o_ref,
              w_u_s, w_v_s, w_c_s, w_out_s, mix_ref):
    rows = x_ref.shape[1]

    @pl.when(_is_first_step())
    def _():
        w_u_s[...] = w_in_ref[:, :D_SGU].astype(BF16)
        w_v_s[...] = w_in_ref[:, D_SGU:].astype(BF16)
        w_out_s[...] = w_out_ref[...].astype(BF16)
        t_idx = lax.broadcasted_iota(jnp.int32, (SGU_CHUNK, SGU_CHUNK), 0)
        s_idx = lax.broadcasted_iota(jnp.int32, (SGU_CHUNK, SGU_CHUNK), 1)
        for grp in range(SGU_GROUPS):
            w_c_s[grp] = jnp.where(t_idx >= s_idx, w_s_ref[grp], 0.0).astype(BF16)

    def front(r0):
        h = _rmsnorm(x_ref[0, r0:r0 + SGU_SUB_ROWS, :], g_ref[...]).astype(BF16)
        v = _gelu_tanh(_dot(h, w_v_s[...]))
        v = _rmsnorm(v, ng_ref[...]).astype(BF16)
        return v, _gelu_tanh(_dot(h, w_u_s[...]))

    def back(r0, v, u):
        chunk_starts = range(0, SGU_SUB_ROWS, SGU_CHUNK)
        for grp in range(SGU_GROUPS):
            lanes = slice(grp * SGU_GROUP_DIM, (grp + 1) * SGU_GROUP_DIM)
            side_by_side = jnp.concatenate([v[c0:c0 + SGU_CHUNK, lanes] for c0 in chunk_starts], axis=1)
            mixed = _dot(w_c_s[grp], side_by_side)
            for n, c0 in enumerate(chunk_starts):
                mix_ref[r0 + c0:r0 + c0 + SGU_CHUNK, lanes] = (
                    mixed[:, n * SGU_GROUP_DIM:(n + 1) * SGU_GROUP_DIM] + bias_ref[:, lanes])
        y = (u * mix_ref[r0:r0 + SGU_SUB_ROWS, :]).astype(BF16)
        o_ref[0, r0:r0 + SGU_SUB_ROWS, :] = x_ref[0, r0:r0 + SGU_SUB_ROWS, :] + _dot(y, w_out_s[...])

    state = front(0)
    for r0 in range(0, rows, SGU_SUB_ROWS):
        state_next = front(r0 + SGU_SUB_ROWS) if r0 + SGU_SUB_ROWS < rows else None
        back(r0, *state)
        state = state_next


def _sgu_layer(x, i, j, g, w_in, norm_g, w_s, s_bias, w_out, mlp_w1, mlp_w2):
    bias = jnp.repeat(jnp.swapaxes(s_bias, 1, 2), SGU_GROUP_DIM, axis=2)
    consts = [(_row(g), i), (w_in, j), (_row(norm_g), j), (w_s, j), (bias, j), (w_out, j)]
    scratch = [
        pltpu.VMEM((D_MODEL, D_SGU), BF16),
        pltpu.VMEM((D_MODEL, D_SGU), BF16),
        pltpu.VMEM((SGU_GROUPS, SGU_CHUNK, SGU_CHUNK), BF16),
        pltpu.VMEM((D_SGU, D_MODEL), BF16),
        pltpu.VMEM((TS_SGU, D_SGU), F32),
    ]
    return _mixer_call(_sgu_body, "sgu_mixer", TS_SGU, x, consts, scratch, mlp_w1, mlp_w2, i)


def _shortconv_body(x_ref, g_ref, w_in_ref, conv_w_ref, w_out_ref, o_ref, w_in_s, w_out_s, hist_ref):
    rows = x_ref.shape[1]

    @pl.when(_is_first_step())
    def _():
        w_in_s[...] = w_in_ref[...].astype(BF16)
        w_out_s[...] = w_out_ref[...].astype(BF16)

    @pl.when(pl.program_id(1) == 0)
    def _():
        hist_ref[0:HIST, :] = jnp.zeros((HIST, D_CONV), F32)

    def front(r0):
        h = _rmsnorm(x_ref[0, r0:r0 + SUB_ROWS, :], g_ref[...]).astype(BF16)
        gb = _dot(h, w_in_s[:, :D_CONV])
        gc = _dot(h, w_in_s[:, D_CONV:2 * D_CONV])
        xv = _dot(h, w_in_s[:, 2 * D_CONV:])
        hist_ref[HIST + r0:HIST + r0 + SUB_ROWS, :] = gc * xv
        conv = None
        for k in range(C_CONV):
            start = HIST + r0 - (C_CONV - 1) + k
            term = conv_w_ref[k:k + 1, :] * hist_ref[start:start + SUB_ROWS, :]
            conv = term if conv is None else conv + term
        return (gb * conv).astype(BF16)

    y = front(0)
    for r0 in range(0, rows, SUB_ROWS):
        y_next = front(r0 + SUB_ROWS) if r0 + SUB_ROWS < rows else None
        o_ref[0, r0:r0 + SUB_ROWS, :] = x_ref[0, r0:r0 + SUB_ROWS, :] + _dot(y, w_out_s[...])
        y = y_next
    hist_ref[0:HIST, :] = hist_ref[rows:rows + HIST, :]


def _shortconv_layer(x, i, j, g, w_in, conv_w, w_out, mlp_w1, mlp_w2):
    consts = [(_row(g), i), (w_in, j), (conv_w, j), (w_out, j)]
    scratch = [
        pltpu.VMEM((D_MODEL, 3 * D_CONV), BF16),
        pltpu.VMEM((D_CONV, D_MODEL), BF16),
        pltpu.VMEM((HIST + TS_SHORTCONV, D_CONV), F32),
    ]
    return _mixer_call(_shortconv_body, "shortconv_mixer", TS_SHORTCONV, x, consts, scratch, mlp_w1, mlp_w2, i)


def _mlp_kernel(x_ref, g_ref, w1_ref, w2_ref, fg_ref, o_ref, *, final_norm):
    for r0 in range(0, x_ref.shape[0], MLP_SUB_ROWS):
        x = x_ref[r0:r0 + MLP_SUB_ROWS, :]
        h = _rmsnorm(x, g_ref[...]).astype(BF16)
        acc = x
        for c in range(D_FF // FF_CHUNK):
            cols = slice(c * FF_CHUNK, (c + 1) * FF_CHUNK)
            a = jnp.square(jnp.maximum(_dot(h, w1_ref[:, cols]), 0.0)).astype(BF16)
            acc = acc + _dot(a, w2_ref[cols, :])
        if final_norm:
            acc = _rmsnorm(acc, fg_ref[...])
        o_ref[r0:r0 + MLP_SUB_ROWS, :] = acc


def _mlp_layer(x2d, i, g, w1_bf16, w2_bf16, final_g, final_norm):
    n, _ = x2d.shape
    tm = TM_MLP
    row_spec = pl.BlockSpec((tm, D_MODEL), lambda r: (r, 0))
    return pl.pallas_call(
        functools.partial(_mlp_kernel, final_norm=final_norm),
        grid=(n // tm,),
        in_specs=[
            row_spec,
            _layer_spec(_row(g), i),
            _layer_spec(w1_bf16[None], 0),
            _layer_spec(w2_bf16[None], 0),
            _layer_spec(final_g[None, None, :], 0),
        ],
        out_specs=row_spec,
        out_shape=jax.ShapeDtypeStruct(x2d.shape, F32),
        compiler_params=pltpu.CompilerParams(
            dimension_semantics=("arbitrary",), vmem_limit_bytes=VMEM_LIMIT),
        name="sqrelu_mlp",
    )(x2d, _row(g), w1_bf16[None], w2_bf16[None], final_g[None, None, :])


def kernel(x, norm_mix_g, norm_mlp_g, final_norm_g, a_w_in, a_conv_w, a_conv_b, a_gate_a_w, a_gate_a_b, a_gate_x_w, a_gate_x_b, a_lambda, a_w_out, b_w_in, b_norm_g, b_w_s, b_s_bias, b_w_out, c_w_in, c_conv_w, c_w_out, mlp_w1, mlp_w2):
    b, s, d = x.shape
    assert (b * s) % TM_MLP == 0 and TM_MLP % MLP_SUB_ROWS == 0 and SUB_ROWS % SGU_CHUNK == 0
    for i in range(DEPTH):
        kind, j = i % N_MIXERS, i // N_MIXERS
        if kind == 0:
            x, w1, w2 = _rglru_layer(x, i, j, norm_mix_g, a_w_in, a_conv_w, a_conv_b, a_gate_a_w,
                                     a_gate_a_b, a_gate_x_w, a_gate_x_b, a_lambda, a_w_out,
                                     mlp_w1, mlp_w2)
        elif kind == 1:
            x, w1, w2 = _sgu_layer(x, i, j, norm_mix_g, b_w_in, b_norm_g, b_w_s, b_s_bias, b_w_out,
                                   mlp_w1, mlp_w2)
        else:
            x, w1, w2 = _shortconv_layer(x, i, j, norm_mix_g, c_w_in, c_conv_w, c_w_out, mlp_w1, mlp_w2)
        x = _mlp_layer(x.reshape(b * s, d), i, norm_mlp_g, w1, w2, final_norm_g,
                       final_norm=(i == DEPTH - 1)).reshape(b, s, d)
    return x
```

```python
import functools

import jax
import jax.numpy as jnp
import numpy as np
from jax import lax
from jax.experimental import pallas as pl
from jax.experimental.pallas import tpu as pltpu

D_MODEL = 1024
DEPTH = 4
N_MIXERS = 3
D_RNN = 1280
A_HEADS = 16
A_HEAD_DIM = D_RNN // A_HEADS
A_CONV = 4
LRU_C = 8.0
D_SGU = D_MODEL
SGU_CHUNK = 128
SGU_GROUPS = 8
SGU_GROUP_DIM = D_SGU // SGU_GROUPS
D_CONV = D_MODEL
C_CONV = 3
D_FF = 4 * D_MODEL
EPS = 1e-6
LOG2_E = 1.4426950408889634

SUBLANES = 8
BF16_ROWS = 16
LANES = 128
MXU_COLS = 256
COL_TILE = 2 * MXU_COLS
A_BAND = MXU_COLS
A_BANDS = D_RNN // A_BAND
A_WINDOW = 2 * A_BAND
HIST = SUBLANES
TS_RGLRU = 512
TS_SGU = 1024
TS_SHORTCONV = 1024
SUB_ROWS = 256
SGU_SUB_ROWS = 512
TM_MLP = 1024
MLP_SUB_ROWS = 512
FF_CHUNK = 2048
VMEM_LIMIT = 56 * 1024 * 1024

F32 = jnp.float32
BF16 = jnp.bfloat16


def _rmsnorm(x, g):
    return x * lax.rsqrt(jnp.mean(x * x, axis=-1, keepdims=True) + EPS) * g


def _dot(a, b):
    return jnp.dot(a, b, preferred_element_type=F32)


def _gelu_tanh(x):
    c = 0.7978845608028654
    half_x = 0.5 * x
    return half_x + half_x * jnp.tanh(x * (c + (c * 0.044715) * (x * x)))


def _layer_spec(stacked, layer):
    zeros = (0,) * (stacked.ndim - 1)
    return pl.BlockSpec((None,) + stacked.shape[1:], lambda *_: (layer,) + zeros,
                        pipeline_mode=pl.Buffered(1))


def _row(stacked):
    return stacked[:, None, :]


def _is_first_step():
    return jnp.logical_and(pl.program_id(0) == 0, pl.program_id(1) == 0)


def _mixer_call(body, name, ts, x, consts, scratch_shapes, mlp_w1, mlp_w2, mlp_layer):
    b, s, _ = x.shape
    assert s % ts == 0 and ts % SUB_ROWS == 0
    nt = s // ts
    steps = b * nt
    row_spec = pl.BlockSpec((1, ts, D_MODEL), lambda i, j: (i, j, 0))

    def slice_rows(w):
        rows = w.shape[1] // steps
        assert rows * steps == w.shape[1] and rows % BF16_ROWS == 0
        return rows

    def in_slice_spec(w):
        return pl.BlockSpec((None, slice_rows(w), w.shape[2]), lambda i, j: (mlp_layer, i * nt + j, 0))

    def out_slice_spec(w):
        return pl.BlockSpec((slice_rows(w), w.shape[2]), lambda i, j: (i * nt + j, 0))

    n_const = len(consts)

    def kernel(x_ref, *refs):
        const_refs = refs[:n_const]
        w1_ref, w2_ref, o_ref, w1_out_ref, w2_out_ref = refs[n_const:n_const + 5]
        w1_out_ref[...] = w1_ref[...].astype(BF16)
        w2_out_ref[...] = w2_ref[...].astype(BF16)
        body(x_ref, *const_refs, o_ref, *refs[n_const + 5:])

    return pl.pallas_call(
        kernel,
        grid=(b, nt),
        in_specs=[row_spec] + [_layer_spec(arr, layer) for arr, layer in consts]
        + [in_slice_spec(mlp_w1), in_slice_spec(mlp_w2)],
        out_specs=[row_spec, out_slice_spec(mlp_w1), out_slice_spec(mlp_w2)],
        out_shape=[jax.ShapeDtypeStruct(x.shape, F32),
                   jax.ShapeDtypeStruct(mlp_w1.shape[1:], BF16),
                   jax.ShapeDtypeStruct(mlp_w2.shape[1:], BF16)],
        scratch_shapes=scratch_shapes,
        compiler_params=pltpu.CompilerParams(
            dimension_semantics=("arbitrary", "arbitrary"), vmem_limit_bytes=VMEM_LIMIT),
        name=name,
    )(x, *[arr for arr, _ in consts], mlp_w1, mlp_w2)


def _sublane_scan(a, b):
    row = lax.broadcasted_iota(jnp.int32, a.shape, 0)
    for d in (1, 2, 4):
        keep = row >= d
        a_prev = jnp.where(keep, pltpu.roll(a, d, 0), 1.0)
        b_prev = jnp.where(keep, pltpu.roll(b, d, 0), 0.0)
        b = a * b_prev + b
        a = a * a_prev
    return a, b


def _segment_permutation(rows, inverse):
    seg = rows // SUBLANES
    i0 = lax.broadcasted_iota(jnp.int32, (rows, rows), 0)
    i1 = lax.broadcasted_iota(jnp.int32, (rows, rows), 1)
    major, natural = (i1, i0) if inverse else (i0, i1)
    hit = natural == (major % SUBLANES) * seg + major // SUBLANES
    return jnp.where(hit, 1.0, 0.0).astype(BF16)


def _band_window_starts():
    starts = []
    for band in range(A_BANDS):
        first_head = (band * A_BAND) // A_HEAD_DIM
        last_head = -(-((band + 1) * A_BAND) // A_HEAD_DIM)
        lo = min((first_head * A_HEAD_DIM) // LANES * LANES, D_RNN - A_WINDOW)
        assert lo <= first_head * A_HEAD_DIM and last_head * A_HEAD_DIM <= lo + A_WINDOW
        starts.append(lo)
    return starts


def _rglru_body(x_ref, g_ref, w_in_ref, conv_w_ref, conv_b_ref, gate_a_ref, gate_x_ref,
                head_tile_ref, head_mask_ref, b_a_ref, b_x_ref, lam_ref, w_out_ref, o_ref,
                w_in_s, w_band_s, w_out_s, ext_ref, tail_ref, xr_ref, a_ref, u_ref, carry_ref):
    window_starts = _band_window_starts()

    @pl.when(_is_first_step())
    def _():
        w_in_s[:, :D_RNN] = w_in_ref[:, D_RNN:].astype(BF16)
        w_in_s[:, D_RNN:] = w_in_ref[:, :D_RNN].astype(BF16)
        w_out_s[...] = w_out_ref[...].astype(BF16)
        for band, lo in enumerate(window_starts):
            for part, gate_ref in enumerate((gate_a_ref, gate_x_ref)):
                tiled = _dot(gate_ref[lo:lo + A_WINDOW, :].astype(BF16), head_tile_ref[band])
                w_band_s[band, :, part * A_BAND:(part + 1) * A_BAND] = (
                    tiled * head_mask_ref[band]).astype(BF16)

    @pl.when(pl.program_id(1) == 0)
    def _():
        tail_ref[...] = jnp.zeros(tail_ref.shape, F32)
        carry_ref[...] = jnp.zeros(carry_ref.shape, F32)

    n_sub = x_ref.shape[1] // SUB_ROWS

    def front(sub):
        x = x_ref[0, sub * SUB_ROWS:(sub + 1) * SUB_ROWS, :]
        return _rglru_front(x, g_ref, conv_w_ref, conv_b_ref, w_in_s,
                            ext_ref.at[sub], tail_ref, xr_ref.at[sub])

    state, _ = _interleave(front(0), None)
    for sub in range(n_sub):
        back = _rglru_back(*state, b_a_ref, b_x_ref, lam_ref, w_band_s, w_out_s,
                           xr_ref.at[sub], a_ref.at[sub], u_ref.at[sub], carry_ref)
        out, state = _interleave(back, front(sub + 1) if sub + 1 < n_sub else None)
        o_ref[0, sub * SUB_ROWS:(sub + 1) * SUB_ROWS, :] = out


def _interleave(first, second):
    results = [None, None]
    live = {k: g for k, g in enumerate((first, second)) if g is not None}
    while live:
        for k in list(live):
            try:
                next(live[k])
            except StopIteration as done:
                results[k] = done.value
                del live[k]
    return results


def _rglru_front(x, g_ref, conv_w_ref, conv_b_ref, w_in_s, ext_ref, tail_ref, xr_ref):
    rows = SUB_ROWS
    lead = (A_CONV - 1) * SUBLANES
    h = _rmsnorm(x, g_ref[...]).astype(BF16)
    hp = _dot(_segment_permutation(rows, False), h).astype(BF16)
    chunks = []
    for c0 in range(0, 2 * D_RNN, COL_TILE):
        chunks.append(_dot(hp, w_in_s[:, c0:c0 + COL_TILE]))
        yield
    proj = jnp.concatenate(chunks, axis=1)
    ext_ref[lead:lead + rows, :] = proj[:, :D_RNN]
    row8 = lax.broadcasted_iota(jnp.int32, (SUBLANES, D_RNN), 0)
    for j in range(A_CONV - 1):
        sl = slice(j * SUBLANES, (j + 1) * SUBLANES)
        cur = ext_ref[rows + j * SUBLANES:rows + (j + 1) * SUBLANES, :]
        ext_ref[sl, :] = jnp.where(row8 == 0, pltpu.roll(tail_ref[sl, :], 1, 0), pltpu.roll(cur, 1, 0))
        tail_ref[sl, :] = cur
    xr_all = conv_b_ref[...]
    for k in range(A_CONV):
        xr_all = xr_all + conv_w_ref[k:k + 1, :] * ext_ref[k * SUBLANES:k * SUBLANES + rows, :]
    xr_ref[...] = xr_all
    return x, xr_all.astype(BF16), proj[:, D_RNN:]


def _rglru_back(x, xr_bf16, gate_pre, b_a_ref, b_x_ref, lam_ref, w_band_s, w_out_s,
                xr_ref, a_ref, u_ref, carry_ref):
    rows = SUB_ROWS
    seg = rows // SUBLANES
    window_starts = _band_window_starts()
    row8 = lax.broadcasted_iota(jnp.int32, (SUBLANES, A_BAND), 0)

    ys = []
    for band, lo in enumerate(window_starts):
        cols = slice(band * A_BAND, (band + 1) * A_BAND)
        xr = xr_ref[:, cols]
        ra = _dot(xr_bf16[:, lo:lo + A_WINDOW], w_band_s[band])
        r = jax.nn.sigmoid(ra[:, :A_BAND] + b_a_ref[:, cols])
        ig = jax.nn.sigmoid(ra[:, A_BAND:] + b_x_ref[:, cols])
        lam = lam_ref[:, cols]
        softplus_neg_lam = jnp.maximum(-lam, 0.0) + jnp.log1p(jnp.exp(-jnp.abs(lam)))
        a = jnp.exp2(r * ((-LRU_C * LOG2_E) * softplus_neg_lam))
        a_ref[:, cols] = a
        z = 1.0 - a * a
        u_ref[:, cols] = jnp.where(z > 0.0, z * lax.rsqrt(z), 0.0) * (ig * xr)

        def step(k):
            sl = slice(k * SUBLANES, (k + 1) * SUBLANES)
            return a_ref[sl, cols], u_ref[sl, cols], sl

        prod = jnp.ones((SUBLANES, A_BAND), F32)
        h_loc = jnp.zeros((SUBLANES, A_BAND), F32)
        for k in range(seg):
            a_k, u_k, _ = step(k)
            h_loc = a_k * h_loc + u_k
            prod = a_k * prod
        c_in = carry_ref[:, cols]
        cum_a, cum_b = _sublane_scan(prod, h_loc)
        seg_end = cum_a * c_in + cum_b
        carry_ref[:, cols] = jnp.broadcast_to(seg_end[SUBLANES - 1:SUBLANES, :], seg_end.shape)
        h_cur = jnp.where(row8 == 0, c_in, pltpu.roll(seg_end, 1, 0))
        for k in range(seg):
            a_k, u_k, sl = step(k)
            h_cur = a_k * h_cur + u_k
            u_ref[sl, cols] = h_cur

        gate = jax.nn.gelu(gate_pre[:, cols])
        ys.append((u_ref[:, cols] * gate).astype(BF16))
        yield

    y = _dot(_segment_permutation(rows, True), jnp.concatenate(ys, axis=1)).astype(BF16)
    return x + _dot(y, w_out_s[...])


def _rglru_layer(x, i, j, g, w_in, conv_w, conv_b, gate_a_w, gate_a_b, gate_x_w, gate_x_b, lam, w_out,
                 mlp_w1, mlp_w2):
    lead = (A_CONV - 1) * SUBLANES
    n_sub = TS_RGLRU // SUB_ROWS
    head_tile = np.zeros((A_BANDS, A_HEAD_DIM, A_BAND), np.float32)
    head_mask = np.zeros((A_BANDS, A_WINDOW, A_BAND), np.float32)
    for band, lo in enumerate(_band_window_starts()):
        col = band * A_BAND + np.arange(A_BAND)
        row = lo + np.arange(A_WINDOW)
        head_tile[band] = col[None, :] % A_HEAD_DIM == np.arange(A_HEAD_DIM)[:, None]
        head_mask[band] = row[:, None] // A_HEAD_DIM == col[None, :] // A_HEAD_DIM
    n_a = w_in.shape[0]
    consts = [
        (_row(g), i), (w_in, j), (conv_w, j), (_row(conv_b), j),
        (gate_a_w.reshape(n_a, D_RNN, A_HEAD_DIM), j), (gate_x_w.reshape(n_a, D_RNN, A_HEAD_DIM), j),
        (jnp.asarray(head_tile, BF16)[None], 0), (jnp.asarray(head_mask, F32)[None], 0),
        (_row(gate_a_b), j), (_row(gate_x_b), j), (_row(lam), j), (w_out, j),
    ]
    scratch = [
        pltpu.VMEM((D_MODEL, 2 * D_RNN), BF16),
        pltpu.VMEM((A_BANDS, A_WINDOW, 2 * A_BAND), BF16),
        pltpu.VMEM((D_RNN, D_MODEL), BF16),
        pltpu.VMEM((n_sub, lead + SUB_ROWS, D_RNN), F32),
        pltpu.VMEM((lead, D_RNN), F32),
        pltpu.VMEM((n_sub, SUB_ROWS, D_RNN), F32),
        pltpu.VMEM((n_sub, SUB_ROWS, D_RNN), F32),
        pltpu.VMEM((n_sub, SUB_ROWS, D_RNN), F32),
        pltpu.VMEM((SUBLANES, D_RNN), F32),
    ]
    return _mixer_call(_rglru_body, "rglru_mixer", TS_RGLRU, x, consts, scratch, mlp_w1, mlp_w2, i)


def _sgu_body(x_ref, g_ref, w_in_ref, ng_ref, w_s_ref, bias_ref, w_out_ref, o_ref,
              w_u_s, w_v_s, w_c_s, w_out_s, mix_ref):
    rows = x_ref.shape[1]

    @pl.when(_is_first_step())
    def _():
        w_u_s[...] = w_in_ref[:, :D_SGU].astype(BF16)
        w_v_s[...] = w_in_ref[:, D_SGU:].astype(BF16)
        w_out_s[...] = w_out_ref[...].astype(BF16)
        t_idx = lax.broadcasted_iota(jnp.int32, (SGU_CHUNK, SGU_CHUNK), 0)
        s_idx = lax.broadcasted_iota(jnp.int32, (SGU_CHUNK, SGU_CHUNK), 1)
        for grp in range(SGU_GROUPS):
            w_c_s[grp] = jnp.where(t_idx >= s_idx, w_s_ref[grp], 0.0).astype(BF16)

    def front(r0):
        h = _rmsnorm(x_ref[0, r0:r0 + SGU_SUB_ROWS, :], g_ref[...]).astype(BF16)
        v = _gelu_tanh(_dot(h, w_v_s[...]))
        v = _rmsnorm(v, ng_ref[...]).astype(BF16)
        return v, _gelu_tanh(_dot(h, w_u_s[...]))

    def back(r0, v, u):
        chunk_starts = range(0, SGU_SUB_ROWS, SGU_CHUNK)
        for grp in range(SGU_GROUPS):
            lanes = slice(grp * SGU_GROUP_DIM, (grp + 1) * SGU_GROUP_DIM)
            side_by_side = jnp.concatenate([v[c0:c0 + SGU_CHUNK, lanes] for c0 in chunk_starts], axis=1)
            mixed = _dot(w_c_s[grp], side_by_side)
            for n, c0 in enumerate(chunk_starts):
                mix_ref[r0 + c0:r0 + c0 + SGU_CHUNK, lanes] = (
                    mixed[:, n * SGU_GROUP_DIM:(n + 1) * SGU_GROUP_DIM] + bias_ref[:, lanes])
        y = (u * mix_ref[r0:r0 + SGU_SUB_ROWS, :]).astype(BF16)
        o_ref[0, r0:r0 + SGU_SUB_ROWS, :] = x_ref[0, r0:r0 + SGU_SUB_ROWS, :] + _dot(y, w_out_s[...])

    state = front(0)
    for r0 in range(0, rows, SGU_SUB_ROWS):
        state_next = front(r0 + SGU_SUB_ROWS) if r0 + SGU_SUB_ROWS < rows else None
        back(r0, *state)
        state = state_next


def _sgu_layer(x, i, j, g, w_in, norm_g, w_s, s_bias, w_out, mlp_w1, mlp_w2):
    bias = jnp.repeat(jnp.swapaxes(s_bias, 1, 2), SGU_GROUP_DIM, axis=2)
    consts = [(_row(g), i), (w_in, j), (_row(norm_g), j), (w_s, j), (bias, j), (w_out, j)]
    scratch = [
        pltpu.VMEM((D_MODEL, D_SGU), BF16),
        pltpu.VMEM((D_MODEL, D_SGU), BF16),
        pltpu.VMEM((SGU_GROUPS, SGU_CHUNK, SGU_CHUNK), BF16),
        pltpu.VMEM((D_SGU, D_MODEL), BF16),
        pltpu.VMEM((TS_SGU, D_SGU), F32),
    ]
    return _mixer_call(_sgu_body, "sgu_mixer", TS_SGU, x, consts, scratch, mlp_w1, mlp_w2, i)


def _shortconv_body(x_ref, g_ref, w_in_ref, conv_w_ref, w_out_ref, o_ref, w_in_s, w_out_s, hist_ref):
    rows = x_ref.shape[1]

    @pl.when(_is_first_step())
    def _():
        w_in_s[...] = w_in_ref[...].astype(BF16)
        w_out_s[...] = w_out_ref[...].astype(BF16)

    @pl.when(pl.program_id(1) == 0)
    def _():
        hist_ref[0:HIST, :] = jnp.zeros((HIST, D_CONV), F32)

    def front(r0):
        h = _rmsnorm(x_ref[0, r0:r0 + SUB_ROWS, :], g_ref[...]).astype(BF16)
        gb = _dot(h, w_in_s[:, :D_CONV])
        gc = _dot(h, w_in_s[:, D_CONV:2 * D_CONV])
        xv = _dot(h, w_in_s[:, 2 * D_CONV:])
        hist_ref[HIST + r0:HIST + r0 + SUB_ROWS, :] = gc * xv
        conv = None
        for k in range(C_CONV):
            start = HIST + r0 - (C_CONV - 1) + k
            term = conv_w_ref[k:k + 1, :] * hist_ref[start:start + SUB_ROWS, :]
            conv = term if conv is None else conv + term
        return (gb * conv).astype(BF16)

    y = front(0)
    for r0 in range(0, rows, SUB_ROWS):
        y_next = front(r0 + SUB_ROWS) if r0 + SUB_ROWS < rows else None
        o_ref[0, r0:r0 + SUB_ROWS, :] = x_ref[0, r0:r0 + SUB_ROWS, :] + _dot(y, w_out_s[...])
        y = y_next
    hist_ref[0:HIST, :] = hist_ref[rows:rows + HIST, :]


def _shortconv_layer(x, i, j, g, w_in, conv_w, w_out, mlp_w1, mlp_w2):
    consts = [(_row(g), i), (w_in, j), (conv_w, j), (w_out, j)]
    scratch = [
        pltpu.VMEM((D_MODEL, 3 * D_CONV), BF16),
        pltpu.VMEM((D_CONV, D_MODEL), BF16),
        pltpu.VMEM((HIST + TS_SHORTCONV, D_CONV), F32),
    ]
    return _mixer_call(_shortconv_body, "shortconv_mixer", TS_SHORTCONV, x, consts, scratch, mlp_w1, mlp_w2, i)


def _mlp_kernel(x_ref, g_ref, w1_ref, w2_ref, fg_ref, o_ref, *, final_norm):
    for r0 in range(0, x_ref.shape[0], MLP_SUB_ROWS):
        x = x_ref[r0:r0 + MLP_SUB_ROWS, :]
        h = _rmsnorm(x, g_ref[...]).astype(BF16)
        acc = x
        for c in range(D_FF // FF_CHUNK):
            cols = slice(c * FF_CHUNK, (c + 1) * FF_CHUNK)
            a = jnp.square(jnp.maximum(_dot(h, w1_ref[:, cols]), 0.0)).astype(BF16)
            acc = acc + _dot(a, w2_ref[cols, :])
        if final_norm:
            acc = _rmsnorm(acc, fg_ref[...])
        o_ref[r0:r0 + MLP_SUB_ROWS, :] = acc


def _mlp_layer(x2d, i, g, w1_bf16, w2_bf16, final_g, final_norm):
    n, _ = x2d.shape
    tm = TM_MLP
    row_spec = pl.BlockSpec((tm, D_MODEL), lambda r: (r, 0))
    return pl.pallas_call(
        functools.partial(_mlp_kernel, final_norm=final_norm),
        grid=(n // tm,),
        in_specs=[
            row_spec,
            _layer_spec(_row(g), i),
            _layer_spec(w1_bf16[None], 0),
            _layer_spec(w2_bf16[None], 0),
            _layer_spec(final_g[None, None, :], 0),
        ],
        out_specs=row_spec,
        out_shape=jax.ShapeDtypeStruct(x2d.shape, F32),
        compiler_params=pltpu.CompilerParams(
            dimension_semantics=("arbitrary",), vmem_limit_bytes=VMEM_LIMIT),
        name="sqrelu_mlp",
    )(x2d, _row(g), w1_bf16[None], w2_bf16[None], final_g[None, None, :])


def kernel(x, norm_mix_g, norm_mlp_g, final_norm_g, a_w_in, a_conv_w, a_conv_b, a_gate_a_w, a_gate_a_b, a_gate_x_w, a_gate_x_b, a_lambda, a_w_out, b_w_in, b_norm_g, b_w_s, b_s_bias, b_w_out, c_w_in, c_conv_w, c_w_out, mlp_w1, mlp_w2):
    b, s, d = x.shape
    assert (b * s) % TM_MLP == 0 and TM_MLP % MLP_SUB_ROWS == 0 and SUB_ROWS % SGU_CHUNK == 0
    for i in range(DEPTH):
        kind, j = i % N_MIXERS, i // N_MIXERS
        if kind == 0:
            x, w1, w2 = _rglru_layer(x, i, j, norm_mix_g, a_w_in, a_conv_w, a_conv_b, a_gate_a_w,
                                     a_gate_a_b, a_gate_x_w, a_gate_x_b, a_lambda, a_w_out,
                                     mlp_w1, mlp_w2)
        elif kind == 1:
            x, w1, w2 = _sgu_layer(x, i, j, norm_mix_g, b_w_in, b_norm_g, b_w_s, b_s_bias, b_w_out,
                                   mlp_w1, mlp_w2)
        else:
            x, w1, w2 = _shortconv_layer(x, i, j, norm_mix_g, c_w_in, c_conv_w, c_w_out, mlp_w1, mlp_w2)
        x = _mlp_layer(x.reshape(b * s, d), i, norm_mlp_g, w1, w2, final_norm_g,
                       final_norm=(i == DEPTH - 1)).reshape(b, s, d)
    return x
```

```python
import functools

import jax
import jax.numpy as jnp
import numpy as np
from jax import lax
from jax.experimental import pallas as pl
from jax.experimental.pallas import tpu as pltpu

D_MODEL = 1024
DEPTH = 4
N_MIXERS = 3
D_RNN = 1280
A_HEADS = 16
A_HEAD_DIM = D_RNN // A_HEADS
A_CONV = 4
LRU_C = 8.0
D_SGU = D_MODEL
SGU_CHUNK = 128
SGU_GROUPS = 8
SGU_GROUP_DIM = D_SGU // SGU_GROUPS
D_CONV = D_MODEL
C_CONV = 3
D_FF = 4 * D_MODEL
EPS = 1e-6
LOG2_E = 1.4426950408889634

SUBLANES = 8
BF16_ROWS = 16
LANES = 128
MXU_COLS = 256
COL_TILE = 2 * MXU_COLS
A_BAND = MXU_COLS
A_BANDS = D_RNN // A_BAND
A_WINDOW = 2 * A_BAND
HIST = SUBLANES
TS_RGLRU = 512
TS_SGU = 1024
TS_SHORTCONV = 1024
SUB_ROWS = 256
SGU_SUB_ROWS = 512
TM_MLP = 1024
MLP_SUB_ROWS = 512
FF_CHUNK = 2048
VMEM_LIMIT = 56 * 1024 * 1024

F32 = jnp.float32
BF16 = jnp.bfloat16


def _rmsnorm(x, g):
    return x * lax.rsqrt(jnp.mean(x * x, axis=-1, keepdims=True) + EPS) * g


def _dot(a, b):
    return jnp.dot(a, b, preferred_element_type=F32)


def _gelu_tanh(x):
    c = 0.7978845608028654
    half_x = 0.5 * x
    return half_x + half_x * jnp.tanh(x * (c + (c * 0.044715) * (x * x)))


def _layer_spec(stacked, layer):
    zeros = (0,) * (stacked.ndim - 1)
    return pl.BlockSpec((None,) + stacked.shape[1:], lambda *_: (layer,) + zeros,
                        pipeline_mode=pl.Buffered(1))


def _row(stacked):
    return stacked[:, None, :]


def _is_first_step():
    return jnp.logical_and(pl.program_id(0) == 0, pl.program_id(1) == 0)


def _mixer_call(body, name, ts, x, consts, scratch_shapes, mlp_w1, mlp_w2, mlp_layer):
    b, s, _ = x.shape
    assert s % ts == 0 and ts % SUB_ROWS == 0
    nt = s // ts
    steps = b * nt
    row_spec = pl.BlockSpec((1, ts, D_MODEL), lambda i, j: (i, j, 0))

    def slice_rows(w):
        rows = w.shape[1] // steps
        assert rows * steps == w.shape[1] and rows % BF16_ROWS == 0
        return rows

    def in_slice_spec(w):
        return pl.BlockSpec((None, slice_rows(w), w.shape[2]), lambda i, j: (mlp_layer, i * nt + j, 0))

    def out_slice_spec(w):
        return pl.BlockSpec((slice_rows(w), w.shape[2]), lambda i, j: (i * nt + j, 0))

    n_const = len(consts)

    def kernel(x_ref, *refs):
        const_refs = refs[:n_const]
        w1_ref, w2_ref, o_ref, w1_out_ref, w2_out_ref = refs[n_const:n_const + 5]
        w1_out_ref[...] = w1_ref[...].astype(BF16)
        w2_out_ref[...] = w2_ref[...].astype(BF16)
        body(x_ref, *const_refs, o_ref, *refs[n_const + 5:])

    return pl.pallas_call(
        kernel,
        grid=(b, nt),
        in_specs=[row_spec] + [_layer_spec(arr, layer) for arr, layer in consts]
        + [in_slice_spec(mlp_w1), in_slice_spec(mlp_w2)],
        out_specs=[row_spec, out_slice_spec(mlp_w1), out_slice_spec(mlp_w2)],
        out_shape=[jax.ShapeDtypeStruct(x.shape, F32),
                   jax.ShapeDtypeStruct(mlp_w1.shape[1:], BF16),
                   jax.ShapeDtypeStruct(mlp_w2.shape[1:], BF16)],
        scratch_shapes=scratch_shapes,
        compiler_params=pltpu.CompilerParams(
            dimension_semantics=("arbitrary", "arbitrary"), vmem_limit_bytes=VMEM_LIMIT),
        name=name,
    )(x, *[arr for arr, _ in consts], mlp_w1, mlp_w2)


def _sublane_scan(a, b):
    row = lax.broadcasted_iota(jnp.int32, a.shape, 0)
    for d in (1, 2, 4):
        keep = row >= d
        a_prev = jnp.where(keep, pltpu.roll(a, d, 0), 1.0)
        b_prev = jnp.where(keep, pltpu.roll(b, d, 0), 0.0)
        b = a * b_prev + b
        a = a * a_prev
    return a, b


def _to_segment_major(v):
    rows, width = v.shape
    return jnp.swapaxes(v.reshape(SUBLANES, rows // SUBLANES, width), 0, 1).reshape(rows, width)


def _from_segment_major(v):
    rows, width = v.shape
    return jnp.swapaxes(v.reshape(rows // SUBLANES, SUBLANES, width), 0, 1).reshape(rows, width)


def _band_window_starts():
    starts = []
    for band in range(A_BANDS):
        first_head = (band * A_BAND) // A_HEAD_DIM
        last_head = -(-((band + 1) * A_BAND) // A_HEAD_DIM)
        lo = min((first_head * A_HEAD_DIM) // LANES * LANES, D_RNN - A_WINDOW)
        assert lo <= first_head * A_HEAD_DIM and last_head * A_HEAD_DIM <= lo + A_WINDOW
        starts.append(lo)
    return starts


def _rglru_body(x_ref, g_ref, w_in_ref, conv_w_ref, conv_b_ref, gate_a_ref, gate_x_ref,
                head_tile_ref, head_mask_ref, b_a_ref, b_x_ref, lam_ref, w_out_ref, o_ref,
                w_in_s, w_band_s, w_out_s, ext_ref, tail_ref, xr_ref, a_ref, u_ref, carry_ref):
    window_starts = _band_window_starts()

    @pl.when(_is_first_step())
    def _():
        w_in_s[:, :D_RNN] = w_in_ref[:, D_RNN:].astype(BF16)
        w_in_s[:, D_RNN:] = w_in_ref[:, :D_RNN].astype(BF16)
        w_out_s[...] = w_out_ref[...].astype(BF16)
        for band, lo in enumerate(window_starts):
            for part, gate_ref in enumerate((gate_a_ref, gate_x_ref)):
                tiled = _dot(gate_ref[lo:lo + A_WINDOW, :].astype(BF16), head_tile_ref[band])
                w_band_s[band, :, part * A_BAND:(part + 1) * A_BAND] = (
                    tiled * head_mask_ref[band]).astype(BF16)

    @pl.when(pl.program_id(1) == 0)
    def _():
        tail_ref[...] = jnp.zeros(tail_ref.shape, F32)
        carry_ref[...] = jnp.zeros(carry_ref.shape, F32)

    n_sub = x_ref.shape[1] // SUB_ROWS

    def front(sub):
        x = x_ref[0, sub * SUB_ROWS:(sub + 1) * SUB_ROWS, :]
        return _rglru_front(x, g_ref, conv_w_ref, conv_b_ref, w_in_s,
                            ext_ref.at[sub], tail_ref, xr_ref.at[sub])

    state, _ = _interleave(front(0), None)
    for sub in range(n_sub):
        back = _rglru_back(*state, b_a_ref, b_x_ref, lam_ref, w_band_s, w_out_s,
                           xr_ref.at[sub], a_ref.at[sub], u_ref.at[sub], carry_ref)
        out, state = _interleave(back, front(sub + 1) if sub + 1 < n_sub else None)
        o_ref[0, sub * SUB_ROWS:(sub + 1) * SUB_ROWS, :] = out


def _interleave(first, second):
    results = [None, None]
    live = {k: g for k, g in enumerate((first, second)) if g is not None}
    while live:
        for k in list(live):
            try:
                next(live[k])
            except StopIteration as done:
                results[k] = done.value
                del live[k]
    return results


def _rglru_front(x, g_ref, conv_w_ref, conv_b_ref, w_in_s, ext_ref, tail_ref, xr_ref):
    rows = SUB_ROWS
    lead = (A_CONV - 1) * SUBLANES
    hp = _to_segment_major(_rmsnorm(x, g_ref[...])).astype(BF16)
    chunks = []
    for c0 in range(0, 2 * D_RNN, COL_TILE):
        chunks.append(_dot(hp, w_in_s[:, c0:c0 + COL_TILE]))
        yield
    proj = jnp.concatenate(chunks, axis=1)
    ext_ref[lead:lead + rows, :] = proj[:, :D_RNN]
    row8 = lax.broadcasted_iota(jnp.int32, (SUBLANES, D_RNN), 0)
    for j in range(A_CONV - 1):
        sl = slice(j * SUBLANES, (j + 1) * SUBLANES)
        cur = ext_ref[rows + j * SUBLANES:rows + (j + 1) * SUBLANES, :]
        ext_ref[sl, :] = jnp.where(row8 == 0, pltpu.roll(tail_ref[sl, :], 1, 0), pltpu.roll(cur, 1, 0))
        tail_ref[sl, :] = cur
    xr_all = conv_b_ref[...]
    for k in range(A_CONV):
        xr_all = xr_all + conv_w_ref[k:k + 1, :] * ext_ref[k * SUBLANES:k * SUBLANES + rows, :]
    xr_ref[...] = xr_all
    return x, xr_all.astype(BF16), proj[:, D_RNN:]


def _rglru_back(x, xr_bf16, gate_pre, b_a_ref, b_x_ref, lam_ref, w_band_s, w_out_s,
                xr_ref, a_ref, u_ref, carry_ref):
    rows = SUB_ROWS
    seg = rows // SUBLANES
    window_starts = _band_window_starts()
    row8 = lax.broadcasted_iota(jnp.int32, (SUBLANES, A_BAND), 0)

    ys = []
    for band, lo in enumerate(window_starts):
        cols = slice(band * A_BAND, (band + 1) * A_BAND)
        xr = xr_ref[:, cols]
        ra = _dot(xr_bf16[:, lo:lo + A_WINDOW], w_band_s[band])
        r = jax.nn.sigmoid(ra[:, :A_BAND] + b_a_ref[:, cols])
        ig = jax.nn.sigmoid(ra[:, A_BAND:] + b_x_ref[:, cols])
        lam = lam_ref[:, cols]
        softplus_neg_lam = jnp.maximum(-lam, 0.0) + jnp.log1p(jnp.exp(-jnp.abs(lam)))
        a = jnp.exp2(r * ((-LRU_C * LOG2_E) * softplus_neg_lam))
        a_ref[:, cols] = a
        z = 1.0 - a * a
        u_ref[:, cols] = jnp.where(z > 0.0, z * lax.rsqrt(z), 0.0) * (ig * xr)

        def step(k):
            sl = slice(k * SUBLANES, (k + 1) * SUBLANES)
            return a_ref[sl, cols], u_ref[sl, cols], sl

        prod = jnp.ones((SUBLANES, A_BAND), F32)
        h_loc = jnp.zeros((SUBLANES, A_BAND), F32)
        for k in range(seg):
            a_k, u_k, _ = step(k)
            h_loc = a_k * h_loc + u_k
            prod = a_k * prod
        c_in = carry_ref[:, cols]
        cum_a, cum_b = _sublane_scan(prod, h_loc)
        seg_end = cum_a * c_in + cum_b
        carry_ref[:, cols] = jnp.broadcast_to(seg_end[SUBLANES - 1:SUBLANES, :], seg_end.shape)
        h_cur = jnp.where(row8 == 0, c_in, pltpu.roll(seg_end, 1, 0))
        for k in range(seg):
            a_k, u_k, sl = step(k)
            h_cur = a_k * h_cur + u_k
            u_ref[sl, cols] = h_cur

        gate = jax.nn.gelu(gate_pre[:, cols])
        ys.append((u_ref[:, cols] * gate).astype(BF16))
        yield

    mixed = _dot(jnp.concatenate(ys, axis=1), w_out_s[...])
    return x + _from_segment_major(mixed)


def _rglru_layer(x, i, j, g, w_in, conv_w, conv_b, gate_a_w, gate_a_b, gate_x_w, gate_x_b, lam, w_out,
                 mlp_w1, mlp_w2):
    lead = (A_CONV - 1) * SUBLANES
    n_sub = TS_RGLRU // SUB_ROWS
    head_tile = np.zeros((A_BANDS, A_HEAD_DIM, A_BAND), np.float32)
    head_mask = np.zeros((A_BANDS, A_WINDOW, A_BAND), np.float32)
    for band, lo in enumerate(_band_window_starts()):
        col = band * A_BAND + np.arange(A_BAND)
        row = lo + np.arange(A_WINDOW)
        head_tile[band] = col[None, :] % A_HEAD_DIM == np.arange(A_HEAD_DIM)[:, None]
        head_mask[band] = row[:, None] // A_HEAD_DIM == col[None, :] // A_HEAD_DIM
    n_a = w_in.shape[0]
    consts = [
        (_row(g), i), (w_in, j), (conv_w, j), (_row(conv_b), j),
        (gate_a_w.reshape(n_a, D_RNN, A_HEAD_DIM), j), (gate_x_w.reshape(n_a, D_RNN, A_HEAD_DIM), j),
        (jnp.asarray(head_tile, BF16)[None], 0), (jnp.asarray(head_mask, F32)[None], 0),
        (_row(gate_a_b), j), (_row(gate_x_b), j), (_row(lam), j), (w_out, j),
    ]
    scratch = [
        pltpu.VMEM((D_MODEL, 2 * D_RNN), BF16),
        pltpu.VMEM((A_BANDS, A_WINDOW, 2 * A_BAND), BF16),
        pltpu.VMEM((D_RNN, D_MODEL), BF16),
        pltpu.VMEM((n_sub, lead + SUB_ROWS, D_RNN), F32),
        pltpu.VMEM((lead, D_RNN), F32),
        pltpu.VMEM((n_sub, SUB_ROWS, D_RNN), F32),
        pltpu.VMEM((n_sub, SUB_ROWS, D_RNN), F32),
        pltpu.VMEM((n_sub, SUB_ROWS, D_RNN), F32),
        pltpu.VMEM((SUBLANES, D_RNN), F32),
    ]
    return _mixer_call(_rglru_body, "rglru_mixer", TS_RGLRU, x, consts, scratch, mlp_w1, mlp_w2, i)


def _sgu_body(x_ref, g_ref, w_in_ref, ng_ref, w_s_ref, bias_ref, w_out_ref, o_ref,
              w_u_s, w_v_s, w_c_s, w_out_s, mix_ref):
    rows = x_ref.shape[1]

    @pl.when(_is_first_step())
    def _():
        w_u_s[...] = w_in_ref[:, :D_SGU].astype(BF16)
        w_v_s[...] = w_in_ref[:, D_SGU:].astype(BF16)
        w_out_s[...] = w_out_ref[...].astype(BF16)
        t_idx = lax.broadcasted_iota(jnp.int32, (SGU_CHUNK, SGU_CHUNK), 0)
        s_idx = lax.broadcasted_iota(jnp.int32, (SGU_CHUNK, SGU_CHUNK), 1)
        for grp in range(SGU_GROUPS):
            w_c_s[grp] = jnp.where(t_idx >= s_idx, w_s_ref[grp], 0.0).astype(BF16)

    def front(r0):
        h = _rmsnorm(x_ref[0, r0:r0 + SGU_SUB_ROWS, :], g_ref[...]).astype(BF16)
        v = _gelu_tanh(_dot(h, w_v_s[...]))
        v = _rmsnorm(v, ng_ref[...]).astype(BF16)
        return v, _gelu_tanh(_dot(h, w_u_s[...]))

    def back(r0, v, u):
        chunk_starts = range(0, SGU_SUB_ROWS, SGU_CHUNK)
        for grp in range(SGU_GROUPS):
            lanes = slice(grp * SGU_GROUP_DIM, (grp + 1) * SGU_GROUP_DIM)
            side_by_side = jnp.concatenate([v[c0:c0 + SGU_CHUNK, lanes] for c0 in chunk_starts], axis=1)
            mixed = _dot(w_c_s[grp], side_by_side)
            for n, c0 in enumerate(chunk_starts):
                mix_ref[r0 + c0:r0 + c0 + SGU_CHUNK, lanes] = (
                    mixed[:, n * SGU_GROUP_DIM:(n + 1) * SGU_GROUP_DIM] + bias_ref[:, lanes])
        y = (u * mix_ref[r0:r0 + SGU_SUB_ROWS, :]).astype(BF16)
        o_ref[0, r0:r0 + SGU_SUB_ROWS, :] = x_ref[0, r0:r0 + SGU_SUB_ROWS, :] + _dot(y, w_out_s[...])

    state = front(0)
    for r0 in range(0, rows, SGU_SUB_ROWS):
        state_next = front(r0 + SGU_SUB_ROWS) if r0 + SGU_SUB_ROWS < rows else None
        back(r0, *state)
        state = state_next


def _sgu_layer(x, i, j, g, w_in, norm_g, w_s, s_bias, w_out, mlp_w1, mlp_w2):
    bias = jnp.repeat(jnp.swapaxes(s_bias, 1, 2), SGU_GROUP_DIM, axis=2)
    consts = [(_row(g), i), (w_in, j), (_row(norm_g), j), (w_s, j), (bias, j), (w_out, j)]
    scratch = [
        pltpu.VMEM((D_MODEL, D_SGU), BF16),
        pltpu.VMEM((D_MODEL, D_SGU), BF16),
        pltpu.VMEM((SGU_GROUPS, SGU_CHUNK, SGU_CHUNK), BF16),
        pltpu.VMEM((D_SGU, D_MODEL), BF16),
        pltpu.VMEM((TS_SGU, D_SGU), F32),
    ]
    return _mixer_call(_sgu_body, "sgu_mixer", TS_SGU, x, consts, scratch, mlp_w1, mlp_w2, i)


def _shortconv_body(x_ref, g_ref, w_in_ref, conv_w_ref, w_out_ref, o_ref, w_in_s, w_out_s, hist_ref):
    rows = x_ref.shape[1]

    @pl.when(_is_first_step())
    def _():
        w_in_s[...] = w_in_ref[...].astype(BF16)
        w_out_s[...] = w_out_ref[...].astype(BF16)

    @pl.when(pl.program_id(1) == 0)
    def _():
        hist_ref[0:HIST, :] = jnp.zeros((HIST, D_CONV), F32)

    def front(r0):
        h = _rmsnorm(x_ref[0, r0:r0 + SUB_ROWS, :], g_ref[...]).astype(BF16)
        gb = _dot(h, w_in_s[:, :D_CONV])
        gc = _dot(h, w_in_s[:, D_CONV:2 * D_CONV])
        xv = _dot(h, w_in_s[:, 2 * D_CONV:])
        hist_ref[HIST + r0:HIST + r0 + SUB_ROWS, :] = gc * xv
        conv = None
        for k in range(C_CONV):
            start = HIST + r0 - (C_CONV - 1) + k
            term = conv_w_ref[k:k + 1, :] * hist_ref[start:start + SUB_ROWS, :]
            conv = term if conv is None else conv + term
        return (gb * conv).astype(BF16)

    y = front(0)
    for r0 in range(0, rows, SUB_ROWS):
        y_next = front(r0 + SUB_ROWS) if r0 + SUB_ROWS < rows else None
        o_ref[0, r0:r0 + SUB_ROWS, :] = x_ref[0, r0:r0 + SUB_ROWS, :] + _dot(y, w_out_s[...])
        y = y_next
    hist_ref[0:HIST, :] = hist_ref[rows:rows + HIST, :]


def _shortconv_layer(x, i, j, g, w_in, conv_w, w_out, mlp_w1, mlp_w2):
    consts = [(_row(g), i), (w_in, j), (conv_w, j), (w_out, j)]
    scratch = [
        pltpu.VMEM((D_MODEL, 3 * D_CONV), BF16),
        pltpu.VMEM((D_CONV, D_MODEL), BF16),
        pltpu.VMEM((HIST + TS_SHORTCONV, D_CONV), F32),
    ]
    return _mixer_call(_shortconv_body, "shortconv_mixer", TS_SHORTCONV, x, consts, scratch, mlp_w1, mlp_w2, i)


def _mlp_kernel(x_ref, g_ref, w1_ref, w2_ref, fg_ref, o_ref, *, final_norm):
    for r0 in range(0, x_ref.shape[0], MLP_SUB_ROWS):
        x = x_ref[r0:r0 + MLP_SUB_ROWS, :]
        h = _rmsnorm(x, g_ref[...]).astype(BF16)
        acc = x
        for c in range(D_FF // FF_CHUNK):
            cols = slice(c * FF_CHUNK, (c + 1) * FF_CHUNK)
            a = jnp.square(jnp.maximum(_dot(h, w1_ref[:, cols]), 0.0)).astype(BF16)
            acc = acc + _dot(a, w2_ref[cols, :])
        if final_norm:
            acc = _rmsnorm(acc, fg_ref[...])
        o_ref[r0:r0 + MLP_SUB_ROWS, :] = acc


def _mlp_layer(x2d, i, g, w1_bf16, w2_bf16, final_g, final_norm):
    n, _ = x2d.shape
    tm = TM_MLP
    row_spec = pl.BlockSpec((tm, D_MODEL), lambda r: (r, 0))
    return pl.pallas_call(
        functools.partial(_mlp_kernel, final_norm=final_norm),
        grid=(n // tm,),
        in_specs=[
            row_spec,
            _layer_spec(_row(g), i),
            _layer_spec(w1_bf16[None], 0),
            _layer_spec(w2_bf16[None], 0),
            _layer_spec(final_g[None, None, :], 0),
        ],
        out_specs=row_spec,
        out_shape=jax.ShapeDtypeStruct(x2d.shape, F32),
        compiler_params=pltpu.CompilerParams(
            dimension_semantics=("arbitrary",), vmem_limit_bytes=VMEM_LIMIT),
        name="sqrelu_mlp",
    )(x2d, _row(g), w1_bf16[None], w2_bf16[None], final_g[None, None, :])


def kernel(x, norm_mix_g, norm_mlp_g, final_norm_g, a_w_in, a_conv_w, a_conv_b, a_gate_a_w, a_gate_a_b, a_gate_x_w, a_gate_x_b, a_lambda, a_w_out, b_w_in, b_norm_g, b_w_s, b_s_bias, b_w_out, c_w_in, c_conv_w, c_w_out, mlp_w1, mlp_w2):
    b, s, d = x.shape
    assert (b * s) % TM_MLP == 0 and TM_MLP % MLP_SUB_ROWS == 0 and SUB_ROWS % SGU_CHUNK == 0
    for i in range(DEPTH):
        kind, j = i % N_MIXERS, i // N_MIXERS
        if kind == 0:
            x, w1, w2 = _rglru_layer(x, i, j, norm_mix_g, a_w_in, a_conv_w, a_conv_b, a_gate_a_w,
                                     a_gate_a_b, a_gate_x_w, a_gate_x_b, a_lambda, a_w_out,
                                     mlp_w1, mlp_w2)
        elif kind == 1:
            x, w1, w2 = _sgu_layer(x, i, j, norm_mix_g, b_w_in, b_norm_g, b_w_s, b_s_bias, b_w_out,
                                   mlp_w1, mlp_w2)
        else:
            x, w1, w2 = _shortconv_layer(x, i, j, norm_mix_g, c_w_in, c_conv_w, c_w_out, mlp_w1, mlp_w2)
        x = _mlp_layer(x.reshape(b * s, d), i, norm_mlp_g, w1, w2, final_norm_g,
                       final_norm=(i == DEPTH - 1)).reshape(b, s, d)
    return x
```

```python
import functools

import jax
import jax.numpy as jnp
import numpy as np
from jax import lax
from jax.experimental import pallas as pl
from jax.experimental.pallas import tpu as pltpu

D_MODEL = 1024
DEPTH = 4
N_MIXERS = 3
D_RNN = 1280
A_HEADS = 16
A_HEAD_DIM = D_RNN // A_HEADS
A_CONV = 4
LRU_C = 8.0
D_SGU = D_MODEL
SGU_CHUNK = 128
SGU_GROUPS = 8
SGU_GROUP_DIM = D_SGU // SGU_GROUPS
D_CONV = D_MODEL
C_CONV = 3
D_FF = 4 * D_MODEL
EPS = 1e-6
LOG2_E = 1.4426950408889634

SUBLANES = 8
BF16_ROWS = 16
LANES = 128
MXU_COLS = 256
COL_TILE = 2 * MXU_COLS
A_BAND = MXU_COLS
A_BANDS = D_RNN // A_BAND
A_WINDOW = 2 * A_BAND
HIST = SUBLANES
TS_RGLRU = 512
TS_SGU = 1024
TS_SHORTCONV = 1024
SUB_ROWS = 256
SGU_SUB_ROWS = 512
TM_MLP = 1024
MLP_SUB_ROWS = 512
FF_CHUNK = 2048
VMEM_LIMIT = 56 * 1024 * 1024

F32 = jnp.float32
BF16 = jnp.bfloat16


def _rmsnorm(x, g):
    return x * lax.rsqrt(jnp.mean(x * x, axis=-1, keepdims=True) + EPS) * g


def _dot(a, b):
    return jnp.dot(a, b, preferred_element_type=F32)


def _gelu_tanh(x):
    c = 0.7978845608028654
    half_x = 0.5 * x
    return half_x + half_x * jnp.tanh(x * (c + (c * 0.044715) * (x * x)))


def _layer_spec(stacked, layer):
    zeros = (0,) * (stacked.ndim - 1)
    return pl.BlockSpec((None,) + stacked.shape[1:], lambda *_: (layer,) + zeros,
                        pipeline_mode=pl.Buffered(1))


def _row(stacked):
    return stacked[:, None, :]


def _is_first_step():
    return jnp.logical_and(pl.program_id(0) == 0, pl.program_id(1) == 0)


def _mixer_call(body, name, ts, x, consts, scratch_shapes, mlp_w1, mlp_w2, mlp_layer):
    b, s, _ = x.shape
    assert s % ts == 0 and ts % SUB_ROWS == 0
    nt = s // ts
    steps = b * nt
    row_spec = pl.BlockSpec((1, ts, D_MODEL), lambda i, j: (i, j, 0))

    def slice_rows(w):
        rows = w.shape[1] // steps
        assert rows * steps == w.shape[1] and rows % BF16_ROWS == 0
        return rows

    def in_slice_spec(w):
        return pl.BlockSpec((None, slice_rows(w), w.shape[2]), lambda i, j: (mlp_layer, i * nt + j, 0))

    def out_slice_spec(w):
        return pl.BlockSpec((slice_rows(w), w.shape[2]), lambda i, j: (i * nt + j, 0))

    n_const = len(consts)

    def kernel(x_ref, *refs):
        const_refs = refs[:n_const]
        w1_ref, w2_ref, o_ref, w1_out_ref, w2_out_ref = refs[n_const:n_const + 5]
        w1_out_ref[...] = w1_ref[...].astype(BF16)
        w2_out_ref[...] = w2_ref[...].astype(BF16)
        body(x_ref, *const_refs, o_ref, *refs[n_const + 5:])

    return pl.pallas_call(
        kernel,
        grid=(b, nt),
        in_specs=[row_spec] + [_layer_spec(arr, layer) for arr, layer in consts]
        + [in_slice_spec(mlp_w1), in_slice_spec(mlp_w2)],
        out_specs=[row_spec, out_slice_spec(mlp_w1), out_slice_spec(mlp_w2)],
        out_shape=[jax.ShapeDtypeStruct(x.shape, F32),
                   jax.ShapeDtypeStruct(mlp_w1.shape[1:], BF16),
                   jax.ShapeDtypeStruct(mlp_w2.shape[1:], BF16)],
        scratch_shapes=scratch_shapes,
        compiler_params=pltpu.CompilerParams(
            dimension_semantics=("arbitrary", "arbitrary"), vmem_limit_bytes=VMEM_LIMIT),
        name=name,
    )(x, *[arr for arr, _ in consts], mlp_w1, mlp_w2)


def _sublane_scan(a, b):
    row = lax.broadcasted_iota(jnp.int32, a.shape, 0)
    for d in (1, 2, 4):
        keep = row >= d
        a_prev = jnp.where(keep, pltpu.roll(a, d, 0), 1.0)
        b_prev = jnp.where(keep, pltpu.roll(b, d, 0), 0.0)
        b = a * b_prev + b
        a = a * a_prev
    return a, b


def _segment_permutation(rows, inverse):
    seg = rows // SUBLANES
    i0 = lax.broadcasted_iota(jnp.int32, (rows, rows), 0)
    i1 = lax.broadcasted_iota(jnp.int32, (rows, rows), 1)
    major, natural = (i1, i0) if inverse else (i0, i1)
    hit = natural == (major % SUBLANES) * seg + major // SUBLANES
    return jnp.where(hit, 1.0, 0.0).astype(BF16)


def _band_window_starts():
    starts = []
    for band in range(A_BANDS):
        first_head = (band * A_BAND) // A_HEAD_DIM
        last_head = -(-((band + 1) * A_BAND) // A_HEAD_DIM)
        lo = min((first_head * A_HEAD_DIM) // LANES * LANES, D_RNN - A_WINDOW)
        assert lo <= first_head * A_HEAD_DIM and last_head * A_HEAD_DIM <= lo + A_WINDOW
        starts.append(lo)
    return starts


def _rglru_body(x_ref, g_ref, w_in_ref, conv_w_ref, conv_b_ref, gate_a_ref, gate_x_ref,
                head_tile_ref, head_mask_ref, b_a_ref, b_x_ref, lam_ref, w_out_ref, o_ref,
                w_in_s, w_band_s, w_out_s, ext_ref, tail_ref, xr_ref, a_ref, u_ref, carry_ref):
    window_starts = _band_window_starts()

    @pl.when(_is_first_step())
    def _():
        w_in_s[:, :D_RNN] = w_in_ref[:, D_RNN:].astype(BF16)
        w_in_s[:, D_RNN:] = w_in_ref[:, :D_RNN].astype(BF16)
        w_out_s[...] = w_out_ref[...].astype(BF16)
        for band, lo in enumerate(window_starts):
            for part, gate_ref in enumerate((gate_a_ref, gate_x_ref)):
                tiled = _dot(gate_ref[lo:lo + A_WINDOW, :].astype(BF16), head_tile_ref[band])
                w_band_s[band, :, part * A_BAND:(part + 1) * A_BAND] = (
                    tiled * head_mask_ref[band]).astype(BF16)

    @pl.when(pl.program_id(1) == 0)
    def _():
        tail_ref[...] = jnp.zeros(tail_ref.shape, F32)
        carry_ref[...] = jnp.zeros(carry_ref.shape, F32)

    n_sub = x_ref.shape[1] // SUB_ROWS

    def front(sub):
        x = x_ref[0, sub * SUB_ROWS:(sub + 1) * SUB_ROWS, :]
        return _rglru_front(x, g_ref, conv_w_ref, conv_b_ref, w_in_s,
                            ext_ref.at[sub], tail_ref, xr_ref.at[sub])

    state, _ = _interleave(front(0), None)
    for sub in range(n_sub):
        back = _rglru_back(*state, b_a_ref, b_x_ref, lam_ref, w_band_s, w_out_s,
                           xr_ref.at[sub], a_ref.at[sub], u_ref.at[sub], carry_ref)
        out, state = _interleave(back, front(sub + 1) if sub + 1 < n_sub else None)
        o_ref[0, sub * SUB_ROWS:(sub + 1) * SUB_ROWS, :] = out


def _interleave(first, second):
    results = [None, None]
    live = {k: g for k, g in enumerate((first, second)) if g is not None}
    while live:
        for k in list(live):
            try:
                next(live[k])
            except StopIteration as done:
                results[k] = done.value
                del live[k]
    return results


def _rglru_front(x, g_ref, conv_w_ref, conv_b_ref, w_in_s, ext_ref, tail_ref, xr_ref):
    rows = SUB_ROWS
    lead = (A_CONV - 1) * SUBLANES
    h = _rmsnorm(x, g_ref[...]).astype(BF16)
    hp = _dot(_segment_permutation(rows, False), h).astype(BF16)
    chunks = []
    for c0 in range(0, 2 * D_RNN, COL_TILE):
        chunks.append(_dot(hp, w_in_s[:, c0:c0 + COL_TILE]))
        yield
    proj = jnp.concatenate(chunks, axis=1)
    ext_ref[lead:lead + rows, :] = proj[:, :D_RNN]
    row8 = lax.broadcasted_iota(jnp.int32, (SUBLANES, D_RNN), 0)
    for j in range(A_CONV - 1):
        sl = slice(j * SUBLANES, (j + 1) * SUBLANES)
        cur = ext_ref[rows + j * SUBLANES:rows + (j + 1) * SUBLANES, :]
        ext_ref[sl, :] = jnp.where(row8 == 0, pltpu.roll(tail_ref[sl, :], 1, 0), pltpu.roll(cur, 1, 0))
        tail_ref[sl, :] = cur
    xr_all = conv_b_ref[...]
    for k in range(A_CONV):
        xr_all = xr_all + conv_w_ref[k:k + 1, :] * ext_ref[k * SUBLANES:k * SUBLANES + rows, :]
    xr_ref[...] = xr_all
    return x, xr_all.astype(BF16), proj[:, D_RNN:]


def _rglru_back(x, xr_bf16, gate_pre, b_a_ref, b_x_ref, lam_ref, w_band_s, w_out_s,
                xr_ref, a_ref, u_ref, carry_ref):
    rows = SUB_ROWS
    seg = rows // SUBLANES
    window_starts = _band_window_starts()
    row8 = lax.broadcasted_iota(jnp.int32, (SUBLANES, A_BAND), 0)

    ys = []
    for band, lo in enumerate(window_starts):
        cols = slice(band * A_BAND, (band + 1) * A_BAND)
        xr = xr_ref[:, cols]
        ra = _dot(xr_bf16[:, lo:lo + A_WINDOW], w_band_s[band])
        r = jax.nn.sigmoid(ra[:, :A_BAND] + b_a_ref[:, cols])
        ig = jax.nn.sigmoid(ra[:, A_BAND:] + b_x_ref[:, cols])
        lam = lam_ref[:, cols]
        softplus_neg_lam = jnp.maximum(-lam, 0.0) + jnp.log1p(jnp.exp(-jnp.abs(lam)))
        a = jnp.exp2(r * ((-LRU_C * LOG2_E) * softplus_neg_lam))
        z = 1.0 - a * a
        u = jnp.where(z > 0.0, z * lax.rsqrt(z), 0.0) * (ig * xr)

        prod = jnp.ones((SUBLANES, A_BAND), F32)
        h_loc = jnp.zeros((SUBLANES, A_BAND), F32)
        for k in range(seg):
            sl = slice(k * SUBLANES, (k + 1) * SUBLANES)
            h_loc = a[sl] * h_loc + u[sl]
            prod = a[sl] * prod
            u_ref[sl, cols] = h_loc
            a_ref[sl, cols] = prod
        c_in = carry_ref[:, cols]
        cum_a, cum_b = _sublane_scan(prod, h_loc)
        seg_end = cum_a * c_in + cum_b
        carry_ref[:, cols] = jnp.broadcast_to(seg_end[SUBLANES - 1:SUBLANES, :], seg_end.shape)
        entry = jnp.where(row8 == 0, c_in, pltpu.roll(seg_end, 1, 0))
        h = u_ref[:, cols] + a_ref[:, cols] * jnp.tile(entry, (seg, 1))

        gate = _gelu_tanh(gate_pre[:, cols])
        ys.append((h * gate).astype(BF16))
        yield

    y = _dot(_segment_permutation(rows, True), jnp.concatenate(ys, axis=1)).astype(BF16)
    return x + _dot(y, w_out_s[...])


def _rglru_layer(x, i, j, g, w_in, conv_w, conv_b, gate_a_w, gate_a_b, gate_x_w, gate_x_b, lam, w_out,
                 mlp_w1, mlp_w2):
    lead = (A_CONV - 1) * SUBLANES
    n_sub = TS_RGLRU // SUB_ROWS
    head_tile = np.zeros((A_BANDS, A_HEAD_DIM, A_BAND), np.float32)
    head_mask = np.zeros((A_BANDS, A_WINDOW, A_BAND), np.float32)
    for band, lo in enumerate(_band_window_starts()):
        col = band * A_BAND + np.arange(A_BAND)
        row = lo + np.arange(A_WINDOW)
        head_tile[band] = col[None, :] % A_HEAD_DIM == np.arange(A_HEAD_DIM)[:, None]
        head_mask[band] = row[:, None] // A_HEAD_DIM == col[None, :] // A_HEAD_DIM
    n_a = w_in.shape[0]
    consts = [
        (_row(g), i), (w_in, j), (conv_w, j), (_row(conv_b), j),
        (gate_a_w.reshape(n_a, D_RNN, A_HEAD_DIM), j), (gate_x_w.reshape(n_a, D_RNN, A_HEAD_DIM), j),
        (jnp.asarray(head_tile, BF16)[None], 0), (jnp.asarray(head_mask, F32)[None], 0),
        (_row(gate_a_b), j), (_row(gate_x_b), j), (_row(lam), j), (w_out, j),
    ]
    scratch = [
        pltpu.VMEM((D_MODEL, 2 * D_RNN), BF16),
        pltpu.VMEM((A_BANDS, A_WINDOW, 2 * A_BAND), BF16),
        pltpu.VMEM((D_RNN, D_MODEL), BF16),
        pltpu.VMEM((n_sub, lead + SUB_ROWS, D_RNN), F32),
        pltpu.VMEM((lead, D_RNN), F32),
        pltpu.VMEM((n_sub, SUB_ROWS, D_RNN), F32),
        pltpu.VMEM((n_sub, SUB_ROWS, D_RNN), F32),
        pltpu.VMEM((n_sub, SUB_ROWS, D_RNN), F32),
        pltpu.VMEM((SUBLANES, D_RNN), F32),
    ]
    return _mixer_call(_rglru_body, "rglru_mixer", TS_RGLRU, x, consts, scratch, mlp_w1, mlp_w2, i)


def _sgu_body(x_ref, g_ref, w_in_ref, ng_ref, w_s_ref, bias_ref, w_out_ref, o_ref,
              w_u_s, w_v_s, w_c_s, w_out_s, mix_ref):
    rows = x_ref.shape[1]

    @pl.when(_is_first_step())
    def _():
        w_u_s[...] = w_in_ref[:, :D_SGU].astype(BF16)
        w_v_s[...] = w_in_ref[:, D_SGU:].astype(BF16)
        w_out_s[...] = w_out_ref[...].astype(BF16)
        t_idx = lax.broadcasted_iota(jnp.int32, (SGU_CHUNK, SGU_CHUNK), 0)
        s_idx = lax.broadcasted_iota(jnp.int32, (SGU_CHUNK, SGU_CHUNK), 1)
        for grp in range(SGU_GROUPS):
            w_c_s[grp] = jnp.where(t_idx >= s_idx, w_s_ref[grp], 0.0).astype(BF16)

    def front(r0):
        h = _rmsnorm(x_ref[0, r0:r0 + SGU_SUB_ROWS, :], g_ref[...]).astype(BF16)
        v = _gelu_tanh(_dot(h, w_v_s[...]))
        v = _rmsnorm(v, ng_ref[...]).astype(BF16)
        return v, _gelu_tanh(_dot(h, w_u_s[...]))

    def back(r0, v, u):
        chunk_starts = range(0, SGU_SUB_ROWS, SGU_CHUNK)
        for grp in range(SGU_GROUPS):
            lanes = slice(grp * SGU_GROUP_DIM, (grp + 1) * SGU_GROUP_DIM)
            side_by_side = jnp.concatenate([v[c0:c0 + SGU_CHUNK, lanes] for c0 in chunk_starts], axis=1)
            mixed = _dot(w_c_s[grp], side_by_side)
            for n, c0 in enumerate(chunk_starts):
                mix_ref[r0 + c0:r0 + c0 + SGU_CHUNK, lanes] = (
                    mixed[:, n * SGU_GROUP_DIM:(n + 1) * SGU_GROUP_DIM] + bias_ref[:, lanes])
        y = (u * mix_ref[r0:r0 + SGU_SUB_ROWS, :]).astype(BF16)
        o_ref[0, r0:r0 + SGU_SUB_ROWS, :] = x_ref[0, r0:r0 + SGU_SUB_ROWS, :] + _dot(y, w_out_s[...])

    state = front(0)
    for r0 in range(0, rows, SGU_SUB_ROWS):
        state_next = front(r0 + SGU_SUB_ROWS) if r0 + SGU_SUB_ROWS < rows else None
        back(r0, *state)
        state = state_next


def _sgu_layer(x, i, j, g, w_in, norm_g, w_s, s_bias, w_out, mlp_w1, mlp_w2):
    bias = jnp.repeat(jnp.swapaxes(s_bias, 1, 2), SGU_GROUP_DIM, axis=2)
    consts = [(_row(g), i), (w_in, j), (_row(norm_g), j), (w_s, j), (bias, j), (w_out, j)]
    scratch = [
        pltpu.VMEM((D_MODEL, D_SGU), BF16),
        pltpu.VMEM((D_MODEL, D_SGU), BF16),
        pltpu.VMEM((SGU_GROUPS, SGU_CHUNK, SGU_CHUNK), BF16),
        pltpu.VMEM((D_SGU, D_MODEL), BF16),
        pltpu.VMEM((TS_SGU, D_SGU), F32),
    ]
    return _mixer_call(_sgu_body, "sgu_mixer", TS_SGU, x, consts, scratch, mlp_w1, mlp_w2, i)


def _shortconv_body(x_ref, g_ref, w_in_ref, conv_w_ref, w_out_ref, o_ref, w_in_s, w_out_s, hist_ref):
    rows = x_ref.shape[1]

    @pl.when(_is_first_step())
    def _():
        w_in_s[...] = w_in_ref[...].astype(BF16)
        w_out_s[...] = w_out_ref[...].astype(BF16)

    @pl.when(pl.program_id(1) == 0)
    def _():
        hist_ref[0:HIST, :] = jnp.zeros((HIST, D_CONV), F32)

    def front(r0):
        h = _rmsnorm(x_ref[0, r0:r0 + SUB_ROWS, :], g_ref[...]).astype(BF16)
        gb = _dot(h, w_in_s[:, :D_CONV])
        gc = _dot(h, w_in_s[:, D_CONV:2 * D_CONV])
        xv = _dot(h, w_in_s[:, 2 * D_CONV:])
        hist_ref[HIST + r0:HIST + r0 + SUB_ROWS, :] = gc * xv
        conv = None
        for k in range(C_CONV):
            start = HIST + r0 - (C_CONV - 1) + k
            term = conv_w_ref[k:k + 1, :] * hist_ref[start:start + SUB_ROWS, :]
            conv = term if conv is None else conv + term
        return (gb * conv).astype(BF16)

    y = front(0)
    for r0 in range(0, rows, SUB_ROWS):
        y_next = front(r0 + SUB_ROWS) if r0 + SUB_ROWS < rows else None
        o_ref[0, r0:r0 + SUB_ROWS, :] = x_ref[0, r0:r0 + SUB_ROWS, :] + _dot(y, w_out_s[...])
        y = y_next
    hist_ref[0:HIST, :] = hist_ref[rows:rows + HIST, :]


def _shortconv_layer(x, i, j, g, w_in, conv_w, w_out, mlp_w1, mlp_w2):
    consts = [(_row(g), i), (w_in, j), (conv_w, j), (w_out, j)]
    scratch = [
        pltpu.VMEM((D_MODEL, 3 * D_CONV), BF16),
        pltpu.VMEM((D_CONV, D_MODEL), BF16),
        pltpu.VMEM((HIST + TS_SHORTCONV, D_CONV), F32),
    ]
    return _mixer_call(_shortconv_body, "shortconv_mixer", TS_SHORTCONV, x, consts, scratch, mlp_w1, mlp_w2, i)


def _mlp_kernel(x_ref, g_ref, w1_ref, w2_ref, fg_ref, o_ref, *, final_norm):
    for r0 in range(0, x_ref.shape[0], MLP_SUB_ROWS):
        x = x_ref[r0:r0 + MLP_SUB_ROWS, :]
        h = _rmsnorm(x, g_ref[...]).astype(BF16)
        acc = x
        for c in range(D_FF // FF_CHUNK):
            cols = slice(c * FF_CHUNK, (c + 1) * FF_CHUNK)
            a = jnp.square(jnp.maximum(_dot(h, w1_ref[:, cols]), 0.0)).astype(BF16)
            acc = acc + _dot(a, w2_ref[cols, :])
        if final_norm:
            acc = _rmsnorm(acc, fg_ref[...])
        o_ref[r0:r0 + MLP_SUB_ROWS, :] = acc


def _mlp_layer(x2d, i, g, w1_bf16, w2_bf16, final_g, final_norm):
    n, _ = x2d.shape
    tm = TM_MLP
    row_spec = pl.BlockSpec((tm, D_MODEL), lambda r: (r, 0))
    return pl.pallas_call(
        functools.partial(_mlp_kernel, final_norm=final_norm),
        grid=(n // tm,),
        in_specs=[
            row_spec,
            _layer_spec(_row(g), i),
            _layer_spec(w1_bf16[None], 0),
            _layer_spec(w2_bf16[None], 0),
            _layer_spec(final_g[None, None, :], 0),
        ],
        out_specs=row_spec,
        out_shape=jax.ShapeDtypeStruct(x2d.shape, F32),
        compiler_params=pltpu.CompilerParams(
            dimension_semantics=("arbitrary",), vmem_limit_bytes=VMEM_LIMIT),
        name="sqrelu_mlp",
    )(x2d, _row(g), w1_bf16[None], w2_bf16[None], final_g[None, None, :])


def kernel(x, norm_mix_g, norm_mlp_g, final_norm_g, a_w_in, a_conv_w, a_conv_b, a_gate_a_w, a_gate_a_b, a_gate_x_w, a_gate_x_b, a_lambda, a_w_out, b_w_in, b_norm_g, b_w_s, b_s_bias, b_w_out, c_w_in, c_conv_w, c_w_out, mlp_w1, mlp_w2):
    b, s, d = x.shape
    assert (b * s) % TM_MLP == 0 and TM_MLP % MLP_SUB_ROWS == 0 and SUB_ROWS % SGU_CHUNK == 0
    for i in range(DEPTH):
        kind, j = i % N_MIXERS, i // N_MIXERS
        if kind == 0:
            x, w1, w2 = _rglru_layer(x, i, j, norm_mix_g, a_w_in, a_conv_w, a_conv_b, a_gate_a_w,
                                     a_gate_a_b, a_gate_x_w, a_gate_x_b, a_lambda, a_w_out,
                                     mlp_w1, mlp_w2)
        elif kind == 1:
            x, w1, w2 = _sgu_layer(x, i, j, norm_mix_g, b_w_in, b_norm_g, b_w_s, b_s_bias, b_w_out,
                                   mlp_w1, mlp_w2)
        else:
            x, w1, w2 = _shortconv_layer(x, i, j, norm_mix_g, c_w_in, c_conv_w, c_w_out, mlp_w1, mlp_w2)
        x = _mlp_layer(x.reshape(b * s, d), i, norm_mlp_g, w1, w2, final_norm_g,
                       final_norm=(i == DEPTH - 1)).reshape(b, s, d)
    return x
```

```python
import functools

import jax
import jax.numpy as jnp
import numpy as np
from jax import lax
from jax.experimental import pallas as pl
from jax.experimental.pallas import tpu as pltpu

D_MODEL = 1024
DEPTH = 4
N_MIXERS = 3
D_RNN = 1280
A_HEADS = 16
A_HEAD_DIM = D_RNN // A_HEADS
A_CONV = 4
LRU_C = 8.0
D_SGU = D_MODEL
SGU_CHUNK = 128
SGU_GROUPS = 8
SGU_GROUP_DIM = D_SGU // SGU_GROUPS
D_CONV = D_MODEL
C_CONV = 3
D_FF = 4 * D_MODEL
EPS = 1e-6
LOG2_E = 1.4426950408889634

SUBLANES = 8
BF16_ROWS = 16
LANES = 128
MXU_COLS = 256
COL_TILE = 2 * MXU_COLS
A_BAND = MXU_COLS
A_BANDS = D_RNN // A_BAND
A_WINDOW = 2 * A_BAND
HIST = SUBLANES
TS_RGLRU = 512
TS_SGU = 1024
TS_SHORTCONV = 1024
SUB_ROWS = 256
C_SUB_ROWS = 512
SGU_SUB_ROWS = 512
TM_MLP = 1024
MLP_SUB_ROWS = 512
VMEM_LIMIT = 56 * 1024 * 1024

F32 = jnp.float32
BF16 = jnp.bfloat16


def _rmsnorm(x, g):
    return x * lax.rsqrt(jnp.mean(x * x, axis=-1, keepdims=True) + EPS) * g


def _dot(a, b):
    return jnp.dot(a, b, preferred_element_type=F32)


def _gelu_tanh(x):
    c = 0.7978845608028654
    half_x = 0.5 * x
    return half_x + half_x * jnp.tanh(x * (c + (c * 0.044715) * (x * x)))


def _layer_spec(stacked, layer):
    zeros = (0,) * (stacked.ndim - 1)
    return pl.BlockSpec((None,) + stacked.shape[1:], lambda *_: (layer,) + zeros,
                        pipeline_mode=pl.Buffered(1))


def _row(stacked):
    return stacked[:, None, :]


def _is_first_step():
    return jnp.logical_and(pl.program_id(0) == 0, pl.program_id(1) == 0)


def _mixer_call(body, name, ts, x, consts, scratch_shapes, mlp_w1, mlp_w2, mlp_layer):
    b, s, _ = x.shape
    assert s % ts == 0 and ts % SUB_ROWS == 0
    nt = s // ts
    steps = b * nt
    row_spec = pl.BlockSpec((1, ts, D_MODEL), lambda i, j: (i, j, 0))

    def slice_rows(w):
        rows = w.shape[1] // steps
        assert rows * steps == w.shape[1] and rows % BF16_ROWS == 0
        return rows

    def in_slice_spec(w):
        return pl.BlockSpec((None, slice_rows(w), w.shape[2]), lambda i, j: (mlp_layer, i * nt + j, 0))

    def out_slice_spec(w):
        return pl.BlockSpec((slice_rows(w), w.shape[2]), lambda i, j: (i * nt + j, 0))

    n_const = len(consts)

    def kernel(x_ref, *refs):
        const_refs = refs[:n_const]
        w1_ref, w2_ref, o_ref, w1_out_ref, w2_out_ref = refs[n_const:n_const + 5]
        w1_out_ref[...] = w1_ref[...].astype(BF16)
        w2_out_ref[...] = w2_ref[...].astype(BF16)
        body(x_ref, *const_refs, o_ref, *refs[n_const + 5:])

    return pl.pallas_call(
        kernel,
        grid=(b, nt),
        in_specs=[row_spec] + [_layer_spec(arr, layer) for arr, layer in consts]
        + [in_slice_spec(mlp_w1), in_slice_spec(mlp_w2)],
        out_specs=[row_spec, out_slice_spec(mlp_w1), out_slice_spec(mlp_w2)],
        out_shape=[jax.ShapeDtypeStruct(x.shape, F32),
                   jax.ShapeDtypeStruct(mlp_w1.shape[1:], BF16),
                   jax.ShapeDtypeStruct(mlp_w2.shape[1:], BF16)],
        scratch_shapes=scratch_shapes,
        compiler_params=pltpu.CompilerParams(
            dimension_semantics=("arbitrary", "arbitrary"), vmem_limit_bytes=VMEM_LIMIT),
        name=name,
    )(x, *[arr for arr, _ in consts], mlp_w1, mlp_w2)


def _sublane_scan(a, b):
    row = lax.broadcasted_iota(jnp.int32, a.shape, 0)
    for d in (1, 2, 4):
        keep = row >= d
        a_prev = jnp.where(keep, pltpu.roll(a, d, 0), 1.0)
        b_prev = jnp.where(keep, pltpu.roll(b, d, 0), 0.0)
        b = a * b_prev + b
        a = a * a_prev
    return a, b


def _segment_permutation(rows, inverse):
    seg = rows // SUBLANES
    i0 = lax.broadcasted_iota(jnp.int32, (rows, rows), 0)
    i1 = lax.broadcasted_iota(jnp.int32, (rows, rows), 1)
    major, natural = (i1, i0) if inverse else (i0, i1)
    hit = natural == (major % SUBLANES) * seg + major // SUBLANES
    return jnp.where(hit, 1.0, 0.0).astype(BF16)


def _band_window_starts():
    starts = []
    for band in range(A_BANDS):
        first_head = (band * A_BAND) // A_HEAD_DIM
        last_head = -(-((band + 1) * A_BAND) // A_HEAD_DIM)
        lo = min((first_head * A_HEAD_DIM) // LANES * LANES, D_RNN - A_WINDOW)
        assert lo <= first_head * A_HEAD_DIM and last_head * A_HEAD_DIM <= lo + A_WINDOW
        starts.append(lo)
    return starts


def _rglru_body(x_ref, g_ref, w_in_ref, conv_w_ref, conv_b_ref, gate_a_ref, gate_x_ref,
                head_tile_ref, head_mask_ref, b_a_ref, b_x_ref, lam_ref, w_out_ref, o_ref,
                w_in_s, w_band_s, w_out_s, ext_ref, tail_ref, xr_ref, a_ref, u_ref, carry_ref):
    window_starts = _band_window_starts()

    @pl.when(_is_first_step())
    def _():
        w_in_s[:, :D_RNN] = w_in_ref[:, D_RNN:].astype(BF16)
        w_in_s[:, D_RNN:] = w_in_ref[:, :D_RNN].astype(BF16)
        w_out_s[...] = w_out_ref[...].astype(BF16)
        for band, lo in enumerate(window_starts):
            for part, gate_ref in enumerate((gate_a_ref, gate_x_ref)):
                tiled = _dot(gate_ref[lo:lo + A_WINDOW, :].astype(BF16), head_tile_ref[band])
                w_band_s[band, :, part * A_BAND:(part + 1) * A_BAND] = (
                    tiled * head_mask_ref[band]).astype(BF16)

    @pl.when(pl.program_id(1) == 0)
    def _():
        tail_ref[...] = jnp.zeros(tail_ref.shape, F32)
        carry_ref[...] = jnp.zeros(carry_ref.shape, F32)

    n_sub = x_ref.shape[1] // SUB_ROWS

    def front(sub):
        x = x_ref[0, sub * SUB_ROWS:(sub + 1) * SUB_ROWS, :]
        return _rglru_front(x, g_ref, conv_w_ref, conv_b_ref, w_in_s,
                            ext_ref.at[sub], tail_ref, xr_ref.at[sub])

    state, _ = _interleave(front(0), None)
    for sub in range(n_sub):
        back = _rglru_back(*state, b_a_ref, b_x_ref, lam_ref, w_band_s, w_out_s,
                           xr_ref.at[sub], a_ref.at[sub], u_ref.at[sub], carry_ref)
        out, state = _interleave(back, front(sub + 1) if sub + 1 < n_sub else None)
        o_ref[0, sub * SUB_ROWS:(sub + 1) * SUB_ROWS, :] = out


def _interleave(first, second):
    results = [None, None]
    live = {k: g for k, g in enumerate((first, second)) if g is not None}
    while live:
        for k in list(live):
            try:
                next(live[k])
            except StopIteration as done:
                results[k] = done.value
                del live[k]
    return results


def _rglru_front(x, g_ref, conv_w_ref, conv_b_ref, w_in_s, ext_ref, tail_ref, xr_ref):
    rows = SUB_ROWS
    lead = (A_CONV - 1) * SUBLANES
    h = _rmsnorm(x, g_ref[...]).astype(BF16)
    hp = _dot(_segment_permutation(rows, False), h).astype(BF16)
    chunks = []
    for c0 in range(0, 2 * D_RNN, COL_TILE):
        chunks.append(_dot(hp, w_in_s[:, c0:c0 + COL_TILE]))
        yield
    proj = jnp.concatenate(chunks, axis=1)
    ext_ref[lead:lead + rows, :] = proj[:, :D_RNN]
    row8 = lax.broadcasted_iota(jnp.int32, (SUBLANES, D_RNN), 0)
    for j in range(A_CONV - 1):
        sl = slice(j * SUBLANES, (j + 1) * SUBLANES)
        cur = ext_ref[rows + j * SUBLANES:rows + (j + 1) * SUBLANES, :]
        ext_ref[sl, :] = jnp.where(row8 == 0, pltpu.roll(tail_ref[sl, :], 1, 0), pltpu.roll(cur, 1, 0))
        tail_ref[sl, :] = cur
    xr_all = conv_b_ref[...]
    for k in range(A_CONV):
        xr_all = xr_all + conv_w_ref[k:k + 1, :] * ext_ref[k * SUBLANES:k * SUBLANES + rows, :]
    xr_ref[...] = xr_all
    return x, xr_all.astype(BF16), proj[:, D_RNN:]


def _rglru_back(x, xr_bf16, gate_pre, b_a_ref, b_x_ref, lam_ref, w_band_s, w_out_s,
                xr_ref, a_ref, u_ref, carry_ref):
    rows = SUB_ROWS
    seg = rows // SUBLANES
    window_starts = _band_window_starts()
    row8 = lax.broadcasted_iota(jnp.int32, (SUBLANES, A_BAND), 0)

    ys = []
    for band, lo in enumerate(window_starts):
        cols = slice(band * A_BAND, (band + 1) * A_BAND)
        xr = xr_ref[:, cols]
        ra = _dot(xr_bf16[:, lo:lo + A_WINDOW], w_band_s[band])
        r = jax.nn.sigmoid(ra[:, :A_BAND] + b_a_ref[:, cols])
        ig = jax.nn.sigmoid(ra[:, A_BAND:] + b_x_ref[:, cols])
        lam = lam_ref[:, cols]
        softplus_neg_lam = jnp.maximum(-lam, 0.0) + jnp.log1p(jnp.exp(-jnp.abs(lam)))
        a = jnp.exp2(r * ((-LRU_C * LOG2_E) * softplus_neg_lam))
        a_ref[:, cols] = a
        z = 1.0 - a * a
        u_ref[:, cols] = jnp.where(z > 0.0, z * lax.rsqrt(z), 0.0) * (ig * xr)

        def step(k):
            sl = slice(k * SUBLANES, (k + 1) * SUBLANES)
            return a_ref[sl, cols], u_ref[sl, cols], sl

        prod = jnp.ones((SUBLANES, A_BAND), F32)
        h_loc = jnp.zeros((SUBLANES, A_BAND), F32)
        for k in range(seg):
            a_k, u_k, _ = step(k)
            h_loc = a_k * h_loc + u_k
            prod = a_k * prod
        c_in = carry_ref[:, cols]
        cum_a, cum_b = _sublane_scan(prod, h_loc)
        seg_end = cum_a * c_in + cum_b
        carry_ref[:, cols] = jnp.broadcast_to(seg_end[SUBLANES - 1:SUBLANES, :], seg_end.shape)
        h_cur = jnp.where(row8 == 0, c_in, pltpu.roll(seg_end, 1, 0))
        for k in range(seg):
            a_k, u_k, sl = step(k)
            h_cur = a_k * h_cur + u_k
            u_ref[sl, cols] = h_cur

        gate = jax.nn.gelu(gate_pre[:, cols])
        ys.append((u_ref[:, cols] * gate).astype(BF16))
        yield

    y = _dot(_segment_permutation(rows, True), jnp.concatenate(ys, axis=1)).astype(BF16)
    return x + _dot(y, w_out_s[...])


def _rglru_layer(x, i, j, g, w_in, conv_w, conv_b, gate_a_w, gate_a_b, gate_x_w, gate_x_b, lam, w_out,
                 mlp_w1, mlp_w2):
    lead = (A_CONV - 1) * SUBLANES
    n_sub = TS_RGLRU // SUB_ROWS
    head_tile = np.zeros((A_BANDS, A_HEAD_DIM, A_BAND), np.float32)
    head_mask = np.zeros((A_BANDS, A_WINDOW, A_BAND), np.float32)
    for band, lo in enumerate(_band_window_starts()):
        col = band * A_BAND + np.arange(A_BAND)
        row = lo + np.arange(A_WINDOW)
        head_tile[band] = col[None, :] % A_HEAD_DIM == np.arange(A_HEAD_DIM)[:, None]
        head_mask[band] = row[:, None] // A_HEAD_DIM == col[None, :] // A_HEAD_DIM
    n_a = w_in.shape[0]
    consts = [
        (_row(g), i), (w_in, j), (conv_w, j), (_row(conv_b), j),
        (gate_a_w.reshape(n_a, D_RNN, A_HEAD_DIM), j), (gate_x_w.reshape(n_a, D_RNN, A_HEAD_DIM), j),
        (jnp.asarray(head_tile, BF16)[None], 0), (jnp.asarray(head_mask, F32)[None], 0),
        (_row(gate_a_b), j), (_row(gate_x_b), j), (_row(lam), j), (w_out, j),
    ]
    scratch = [
        pltpu.VMEM((D_MODEL, 2 * D_RNN), BF16),
        pltpu.VMEM((A_BANDS, A_WINDOW, 2 * A_BAND), BF16),
        pltpu.VMEM((D_RNN, D_MODEL), BF16),
        pltpu.VMEM((n_sub, lead + SUB_ROWS, D_RNN), F32),
        pltpu.VMEM((lead, D_RNN), F32),
        pltpu.VMEM((n_sub, SUB_ROWS, D_RNN), F32),
        pltpu.VMEM((n_sub, SUB_ROWS, D_RNN), F32),
        pltpu.VMEM((n_sub, SUB_ROWS, D_RNN), F32),
        pltpu.VMEM((SUBLANES, D_RNN), F32),
    ]
    return _mixer_call(_rglru_body, "rglru_mixer", TS_RGLRU, x, consts, scratch, mlp_w1, mlp_w2, i)


def _sgu_body(x_ref, g_ref, w_in_ref, ng_ref, w_s_ref, bias_ref, w_out_ref, o_ref,
              w_u_s, w_v_s, w_c_s, w_out_s, mix_ref):
    rows = x_ref.shape[1]

    @pl.when(_is_first_step())
    def _():
        w_u_s[...] = w_in_ref[:, :D_SGU].astype(BF16)
        w_v_s[...] = w_in_ref[:, D_SGU:].astype(BF16)
        w_out_s[...] = w_out_ref[...].astype(BF16)
        t_idx = lax.broadcasted_iota(jnp.int32, (SGU_CHUNK, SGU_CHUNK), 0)
        s_idx = lax.broadcasted_iota(jnp.int32, (SGU_CHUNK, SGU_CHUNK), 1)
        for grp in range(SGU_GROUPS):
            w_c_s[grp] = jnp.where(t_idx >= s_idx, w_s_ref[grp], 0.0).astype(BF16)

    def front(r0):
        h = _rmsnorm(x_ref[0, r0:r0 + SGU_SUB_ROWS, :], g_ref[...]).astype(BF16)
        v = _gelu_tanh(_dot(h, w_v_s[...]))
        v = _rmsnorm(v, ng_ref[...]).astype(BF16)
        return v, _gelu_tanh(_dot(h, w_u_s[...]))

    def back(r0, v, u):
        chunk_starts = range(0, SGU_SUB_ROWS, SGU_CHUNK)
        for grp in range(SGU_GROUPS):
            lanes = slice(grp * SGU_GROUP_DIM, (grp + 1) * SGU_GROUP_DIM)
            side_by_side = jnp.concatenate([v[c0:c0 + SGU_CHUNK, lanes] for c0 in chunk_starts], axis=1)
            mixed = _dot(w_c_s[grp], side_by_side)
            for n, c0 in enumerate(chunk_starts):
                mix_ref[r0 + c0:r0 + c0 + SGU_CHUNK, lanes] = (
                    mixed[:, n * SGU_GROUP_DIM:(n + 1) * SGU_GROUP_DIM] + bias_ref[:, lanes])
        y = (u * mix_ref[r0:r0 + SGU_SUB_ROWS, :]).astype(BF16)
        o_ref[0, r0:r0 + SGU_SUB_ROWS, :] = x_ref[0, r0:r0 + SGU_SUB_ROWS, :] + _dot(y, w_out_s[...])

    state = front(0)
    for r0 in range(0, rows, SGU_SUB_ROWS):
        state_next = front(r0 + SGU_SUB_ROWS) if r0 + SGU_SUB_ROWS < rows else None
        back(r0, *state)
        state = state_next


def _sgu_layer(x, i, j, g, w_in, norm_g, w_s, s_bias, w_out, mlp_w1, mlp_w2):
    bias = jnp.repeat(jnp.swapaxes(s_bias, 1, 2), SGU_GROUP_DIM, axis=2)
    consts = [(_row(g), i), (w_in, j), (_row(norm_g), j), (w_s, j), (bias, j), (w_out, j)]
    scratch = [
        pltpu.VMEM((D_MODEL, D_SGU), BF16),
        pltpu.VMEM((D_MODEL, D_SGU), BF16),
        pltpu.VMEM((SGU_GROUPS, SGU_CHUNK, SGU_CHUNK), BF16),
        pltpu.VMEM((D_SGU, D_MODEL), BF16),
        pltpu.VMEM((TS_SGU, D_SGU), F32),
    ]
    return _mixer_call(_sgu_body, "sgu_mixer", TS_SGU, x, consts, scratch, mlp_w1, mlp_w2, i)


def _shortconv_body(x_ref, g_ref, w_in_ref, conv_w_ref, w_out_ref, o_ref, w_in_s, w_out_s, hist_ref):
    rows = x_ref.shape[1]

    @pl.when(_is_first_step())
    def _():
        w_in_s[...] = w_in_ref[...].astype(BF16)
        w_out_s[...] = w_out_ref[...].astype(BF16)

    @pl.when(pl.program_id(1) == 0)
    def _():
        hist_ref[0:HIST, :] = jnp.zeros((HIST, D_CONV), F32)

    def front(r0):
        h = _rmsnorm(x_ref[0, r0:r0 + C_SUB_ROWS, :], g_ref[...]).astype(BF16)
        gb = _dot(h, w_in_s[:, :D_CONV])
        gc = _dot(h, w_in_s[:, D_CONV:2 * D_CONV])
        xv = _dot(h, w_in_s[:, 2 * D_CONV:])
        hist_ref[HIST + r0:HIST + r0 + C_SUB_ROWS, :] = gc * xv
        conv = None
        for k in range(C_CONV):
            start = HIST + r0 - (C_CONV - 1) + k
            term = conv_w_ref[k:k + 1, :] * hist_ref[start:start + C_SUB_ROWS, :]
            conv = term if conv is None else conv + term
        return (gb * conv).astype(BF16)

    y = front(0)
    for r0 in range(0, rows, C_SUB_ROWS):
        y_next = front(r0 + C_SUB_ROWS) if r0 + C_SUB_ROWS < rows else None
        o_ref[0, r0:r0 + C_SUB_ROWS, :] = x_ref[0, r0:r0 + C_SUB_ROWS, :] + _dot(y, w_out_s[...])
        y = y_next
    hist_ref[0:HIST, :] = hist_ref[rows:rows + HIST, :]


def _shortconv_layer(x, i, j, g, w_in, conv_w, w_out, mlp_w1, mlp_w2):
    consts = [(_row(g), i), (w_in, j), (conv_w, j), (w_out, j)]
    scratch = [
        pltpu.VMEM((D_MODEL, 3 * D_CONV), BF16),
        pltpu.VMEM((D_CONV, D_MODEL), BF16),
        pltpu.VMEM((HIST + TS_SHORTCONV, D_CONV), F32),
    ]
    return _mixer_call(_shortconv_body, "shortconv_mixer", TS_SHORTCONV, x, consts, scratch, mlp_w1, mlp_w2, i)


def _mlp_kernel(x_ref, g_ref, w1_ref, w2_ref, fg_ref, o_ref, *, final_norm):
    for r0 in range(0, x_ref.shape[0], MLP_SUB_ROWS):
        x = x_ref[r0:r0 + MLP_SUB_ROWS, :]
        h = _rmsnorm(x, g_ref[...]).astype(BF16)
        a = jnp.square(jnp.maximum(_dot(h, w1_ref[...]), 0.0)).astype(BF16)
        acc = x + _dot(a, w2_ref[...])
        if final_norm:
            acc = _rmsnorm(acc, fg_ref[...])
        o_ref[r0:r0 + MLP_SUB_ROWS, :] = acc


def _mlp_layer(x2d, i, g, w1_bf16, w2_bf16, final_g, final_norm):
    n, _ = x2d.shape
    tm = TM_MLP
    row_spec = pl.BlockSpec((tm, D_MODEL), lambda r: (r, 0))
    return pl.pallas_call(
        functools.partial(_mlp_kernel, final_norm=final_norm),
        grid=(n // tm,),
        in_specs=[
            row_spec,
            _layer_spec(_row(g), i),
            _layer_spec(w1_bf16[None], 0),
            _layer_spec(w2_bf16[None], 0),
            _layer_spec(final_g[None, None, :], 0),
        ],
        out_specs=row_spec,
        out_shape=jax.ShapeDtypeStruct(x2d.shape, F32),
        compiler_params=pltpu.CompilerParams(
            dimension_semantics=("arbitrary",), vmem_limit_bytes=VMEM_LIMIT),
        name="sqrelu_mlp",
    )(x2d, _row(g), w1_bf16[None], w2_bf16[None], final_g[None, None, :])


def kernel(x, norm_mix_g, norm_mlp_g, final_norm_g, a_w_in, a_conv_w, a_conv_b, a_gate_a_w, a_gate_a_b, a_gate_x_w, a_gate_x_b, a_lambda, a_w_out, b_w_in, b_norm_g, b_w_s, b_s_bias, b_w_out, c_w_in, c_conv_w, c_w_out, mlp_w1, mlp_w2):
    b, s, d = x.shape
    assert (b * s) % TM_MLP == 0 and TM_MLP % MLP_SUB_ROWS == 0 and SUB_ROWS % SGU_CHUNK == 0
    for i in range(DEPTH):
        kind, j = i % N_MIXERS, i // N_MIXERS
        if kind == 0:
            x, w1, w2 = _rglru_layer(x, i, j, norm_mix_g, a_w_in, a_conv_w, a_conv_b, a_gate_a_w,
                                     a_gate_a_b, a_gate_x_w, a_gate_x_b, a_lambda, a_w_out,
                                     mlp_w1, mlp_w2)
        elif kind == 1:
            x, w1, w2 = _sgu_layer(x, i, j, norm_mix_g, b_w_in, b_norm_g, b_w_s, b_s_bias, b_w_out,
                                   mlp_w1, mlp_w2)
        else:
            x, w1, w2 = _shortconv_layer(x, i, j, norm_mix_g, c_w_in, c_conv_w, c_w_out, mlp_w1, mlp_w2)
        x = _mlp_layer(x.reshape(b * s, d), i, norm_mlp_g, w1, w2, final_norm_g,
                       final_norm=(i == DEPTH - 1)).reshape(b, s, d)
    return x
```

```python
import functools

import jax
import jax.numpy as jnp
import numpy as np
from jax import lax
from jax.experimental import pallas as pl
from jax.experimental.pallas import tpu as pltpu

D_MODEL = 1024
DEPTH = 4
N_MIXERS = 3
D_RNN = 1280
A_HEADS = 16
A_HEAD_DIM = D_RNN // A_HEADS
A_CONV = 4
LRU_C = 8.0
D_SGU = D_MODEL
SGU_CHUNK = 128
SGU_GROUPS = 8
SGU_GROUP_DIM = D_SGU // SGU_GROUPS
D_CONV = D_MODEL
C_CONV = 3
D_FF = 4 * D_MODEL
EPS = 1e-6
LOG2_E = 1.4426950408889634

SUBLANES = 8
BF16_ROWS = 16
LANES = 128
MXU_COLS = 256
COL_TILE = 2 * MXU_COLS
A_BAND = MXU_COLS
A_BANDS = D_RNN // A_BAND
A_WINDOW = 2 * A_BAND
HIST = SUBLANES
TS_RGLRU = 512
TS_SGU = 1024
TS_SHORTCONV = 1024
SUB_ROWS = 256
C_SUB_ROWS = 512
SGU_SUB_ROWS = 512
TM_MLP = 1024
MLP_SUB_ROWS = 512
VMEM_LIMIT = 56 * 1024 * 1024

F32 = jnp.float32
BF16 = jnp.bfloat16


def _rmsnorm(x, g):
    return x * lax.rsqrt(jnp.mean(x * x, axis=-1, keepdims=True) + EPS) * g


def _dot(a, b):
    return jnp.dot(a, b, preferred_element_type=F32)


def _sigmoid(x):
    return 0.5 + 0.5 * jnp.tanh(0.5 * x)


def _gelu_tanh(x):
    c = 0.7978845608028654
    half_x = 0.5 * x
    return half_x + half_x * jnp.tanh(x * (c + (c * 0.044715) * (x * x)))


def _layer_spec(stacked, layer):
    zeros = (0,) * (stacked.ndim - 1)
    return pl.BlockSpec((None,) + stacked.shape[1:], lambda *_: (layer,) + zeros,
                        pipeline_mode=pl.Buffered(1))


def _row(stacked):
    return stacked[:, None, :]


def _is_first_step():
    return jnp.logical_and(pl.program_id(0) == 0, pl.program_id(1) == 0)


def _mixer_call(body, name, ts, x, consts, scratch_shapes, mlp_w1, mlp_w2, mlp_layer):
    b, s, _ = x.shape
    assert s % ts == 0 and ts % SUB_ROWS == 0
    nt = s // ts
    steps = b * nt
    row_spec = pl.BlockSpec((1, ts, D_MODEL), lambda i, j: (i, j, 0))

    def slice_rows(w):
        rows = w.shape[1] // steps
        assert rows * steps == w.shape[1] and rows % BF16_ROWS == 0
        return rows

    def in_slice_spec(w):
        return pl.BlockSpec((None, slice_rows(w), w.shape[2]), lambda i, j: (mlp_layer, i * nt + j, 0))

    def out_slice_spec(w):
        return pl.BlockSpec((slice_rows(w), w.shape[2]), lambda i, j: (i * nt + j, 0))

    n_const = len(consts)

    def kernel(x_ref, *refs):
        const_refs = refs[:n_const]
        w1_ref, w2_ref, o_ref, w1_out_ref, w2_out_ref = refs[n_const:n_const + 5]
        w1_out_ref[...] = w1_ref[...].astype(BF16)
        w2_out_ref[...] = w2_ref[...].astype(BF16)
        body(x_ref, *const_refs, o_ref, *refs[n_const + 5:])

    return pl.pallas_call(
        kernel,
        grid=(b, nt),
        in_specs=[row_spec] + [_layer_spec(arr, layer) for arr, layer in consts]
        + [in_slice_spec(mlp_w1), in_slice_spec(mlp_w2)],
        out_specs=[row_spec, out_slice_spec(mlp_w1), out_slice_spec(mlp_w2)],
        out_shape=[jax.ShapeDtypeStruct(x.shape, F32),
                   jax.ShapeDtypeStruct(mlp_w1.shape[1:], BF16),
                   jax.ShapeDtypeStruct(mlp_w2.shape[1:], BF16)],
        scratch_shapes=scratch_shapes,
        compiler_params=pltpu.CompilerParams(
            dimension_semantics=("arbitrary", "arbitrary"), vmem_limit_bytes=VMEM_LIMIT),
        name=name,
    )(x, *[arr for arr, _ in consts], mlp_w1, mlp_w2)


def _sublane_scan(a, b):
    row = lax.broadcasted_iota(jnp.int32, a.shape, 0)
    for d in (1, 2, 4):
        keep = row >= d
        a_prev = jnp.where(keep, pltpu.roll(a, d, 0), 1.0)
        b_prev = jnp.where(keep, pltpu.roll(b, d, 0), 0.0)
        b = a * b_prev + b
        a = a * a_prev
    return a, b


def _segment_permutation(rows, inverse):
    seg = rows // SUBLANES
    i0 = lax.broadcasted_iota(jnp.int32, (rows, rows), 0)
    i1 = lax.broadcasted_iota(jnp.int32, (rows, rows), 1)
    major, natural = (i1, i0) if inverse else (i0, i1)
    hit = natural == (major % SUBLANES) * seg + major // SUBLANES
    return jnp.where(hit, 1.0, 0.0).astype(BF16)


def _band_window_starts():
    starts = []
    for band in range(A_BANDS):
        first_head = (band * A_BAND) // A_HEAD_DIM
        last_head = -(-((band + 1) * A_BAND) // A_HEAD_DIM)
        lo = min((first_head * A_HEAD_DIM) // LANES * LANES, D_RNN - A_WINDOW)
        assert lo <= first_head * A_HEAD_DIM and last_head * A_HEAD_DIM <= lo + A_WINDOW
        starts.append(lo)
    return starts


def _rglru_body(x_ref, g_ref, w_in_ref, conv_w_ref, conv_b_ref, gate_a_ref, gate_x_ref,
                head_tile_ref, head_mask_ref, b_a_ref, b_x_ref, lam_ref, w_out_ref, o_ref,
                w_in_s, w_band_s, w_out_s, ext_ref, tail_ref, xr_ref, a_ref, u_ref, carry_ref):
    window_starts = _band_window_starts()

    @pl.when(_is_first_step())
    def _():
        w_in_s[:, :D_RNN] = w_in_ref[:, D_RNN:].astype(BF16)
        w_in_s[:, D_RNN:] = w_in_ref[:, :D_RNN].astype(BF16)
        w_out_s[...] = w_out_ref[...].astype(BF16)
        for band, lo in enumerate(window_starts):
            for part, gate_ref in enumerate((gate_a_ref, gate_x_ref)):
                tiled = _dot(gate_ref[lo:lo + A_WINDOW, :].astype(BF16), head_tile_ref[band])
                w_band_s[band, :, part * A_BAND:(part + 1) * A_BAND] = (
                    tiled * head_mask_ref[band]).astype(BF16)

    @pl.when(pl.program_id(1) == 0)
    def _():
        tail_ref[...] = jnp.zeros(tail_ref.shape, F32)
        carry_ref[...] = jnp.zeros(carry_ref.shape, F32)

    n_sub = x_ref.shape[1] // SUB_ROWS

    def front(sub):
        x = x_ref[0, sub * SUB_ROWS:(sub + 1) * SUB_ROWS, :]
        return _rglru_front(x, g_ref, conv_w_ref, conv_b_ref, w_in_s,
                            ext_ref.at[sub], tail_ref, xr_ref.at[sub])

    state, _ = _interleave(front(0), None)
    for sub in range(n_sub):
        back = _rglru_back(*state, b_a_ref, b_x_ref, lam_ref, w_band_s, w_out_s,
                           xr_ref.at[sub], a_ref.at[sub], u_ref.at[sub], carry_ref)
        out, state = _interleave(back, front(sub + 1) if sub + 1 < n_sub else None)
        o_ref[0, sub * SUB_ROWS:(sub + 1) * SUB_ROWS, :] = out


def _interleave(first, second):
    results = [None, None]
    live = {k: g for k, g in enumerate((first, second)) if g is not None}
    while live:
        for k in list(live):
            try:
                next(live[k])
            except StopIteration as done:
                results[k] = done.value
                del live[k]
    return results


def _rglru_front(x, g_ref, conv_w_ref, conv_b_ref, w_in_s, ext_ref, tail_ref, xr_ref):
    rows = SUB_ROWS
    lead = (A_CONV - 1) * SUBLANES
    h = _rmsnorm(x, g_ref[...]).astype(BF16)
    hp = _dot(_segment_permutation(rows, False), h).astype(BF16)
    chunks = []
    for c0 in range(0, 2 * D_RNN, COL_TILE):
        chunks.append(_dot(hp, w_in_s[:, c0:c0 + COL_TILE]))
        yield
    proj = jnp.concatenate(chunks, axis=1)
    ext_ref[lead:lead + rows, :] = proj[:, :D_RNN]
    row8 = lax.broadcasted_iota(jnp.int32, (SUBLANES, D_RNN), 0)
    for j in range(A_CONV - 1):
        sl = slice(j * SUBLANES, (j + 1) * SUBLANES)
        cur = ext_ref[rows + j * SUBLANES:rows + (j + 1) * SUBLANES, :]
        ext_ref[sl, :] = jnp.where(row8 == 0, pltpu.roll(tail_ref[sl, :], 1, 0), pltpu.roll(cur, 1, 0))
        tail_ref[sl, :] = cur
    xr_all = conv_b_ref[...]
    for k in range(A_CONV):
        xr_all = xr_all + conv_w_ref[k:k + 1, :] * ext_ref[k * SUBLANES:k * SUBLANES + rows, :]
    xr_ref[...] = xr_all
    return x, xr_all.astype(BF16), proj[:, D_RNN:]


def _rglru_back(x, xr_bf16, gate_pre, b_a_ref, b_x_ref, lam_ref, w_band_s, w_out_s,
                xr_ref, a_ref, u_ref, carry_ref):
    rows = SUB_ROWS
    seg = rows // SUBLANES
    window_starts = _band_window_starts()
    row8 = lax.broadcasted_iota(jnp.int32, (SUBLANES, A_BAND), 0)

    ys = []
    for band, lo in enumerate(window_starts):
        cols = slice(band * A_BAND, (band + 1) * A_BAND)
        xr = xr_ref[:, cols]
        ra = _dot(xr_bf16[:, lo:lo + A_WINDOW], w_band_s[band])
        r = _sigmoid(ra[:, :A_BAND] + b_a_ref[:, cols])
        ig = _sigmoid(ra[:, A_BAND:] + b_x_ref[:, cols])
        lam = lam_ref[:, cols]
        softplus_neg_lam = jnp.maximum(-lam, 0.0) + jnp.log1p(jnp.exp(-jnp.abs(lam)))
        a = jnp.exp2(r * ((-LRU_C * LOG2_E) * softplus_neg_lam))
        a_ref[:, cols] = a
        z = 1.0 - a * a
        u_ref[:, cols] = jnp.where(z > 0.0, z * lax.rsqrt(z), 0.0) * (ig * xr)

        def step(k):
            sl = slice(k * SUBLANES, (k + 1) * SUBLANES)
            return a_ref[sl, cols], u_ref[sl, cols], sl

        prod = jnp.ones((SUBLANES, A_BAND), F32)
        h_loc = jnp.zeros((SUBLANES, A_BAND), F32)
        for k in range(seg):
            a_k, u_k, _ = step(k)
            h_loc = a_k * h_loc + u_k
            prod = a_k * prod
        c_in = carry_ref[:, cols]
        cum_a, cum_b = _sublane_scan(prod, h_loc)
        seg_end = cum_a * c_in + cum_b
        carry_ref[:, cols] = jnp.broadcast_to(seg_end[SUBLANES - 1:SUBLANES, :], seg_end.shape)
        h_cur = jnp.where(row8 == 0, c_in, pltpu.roll(seg_end, 1, 0))
        for k in range(seg):
            a_k, u_k, sl = step(k)
            h_cur = a_k * h_cur + u_k
            u_ref[sl, cols] = h_cur

        gate = jax.nn.gelu(gate_pre[:, cols])
        ys.append((u_ref[:, cols] * gate).astype(BF16))
        yield

    y = _dot(_segment_permutation(rows, True), jnp.concatenate(ys, axis=1)).astype(BF16)
    return x + _dot(y, w_out_s[...])


def _rglru_layer(x, i, j, g, w_in, conv_w, conv_b, gate_a_w, gate_a_b, gate_x_w, gate_x_b, lam, w_out,
                 mlp_w1, mlp_w2):
    lead = (A_CONV - 1) * SUBLANES
    n_sub = TS_RGLRU // SUB_ROWS
    head_tile = np.zeros((A_BANDS, A_HEAD_DIM, A_BAND), np.float32)
    head_mask = np.zeros((A_BANDS, A_WINDOW, A_BAND), np.float32)
    for band, lo in enumerate(_band_window_starts()):
        col = band * A_BAND + np.arange(A_BAND)
        row = lo + np.arange(A_WINDOW)
        head_tile[band] = col[None, :] % A_HEAD_DIM == np.arange(A_HEAD_DIM)[:, None]
        head_mask[band] = row[:, None] // A_HEAD_DIM == col[None, :] // A_HEAD_DIM
    n_a = w_in.shape[0]
    consts = [
        (_row(g), i), (w_in, j), (conv_w, j), (_row(conv_b), j),
        (gate_a_w.reshape(n_a, D_RNN, A_HEAD_DIM), j), (gate_x_w.reshape(n_a, D_RNN, A_HEAD_DIM), j),
        (jnp.asarray(head_tile, BF16)[None], 0), (jnp.asarray(head_mask, F32)[None], 0),
        (_row(gate_a_b), j), (_row(gate_x_b), j), (_row(lam), j), (w_out, j),
    ]
    scratch = [
        pltpu.VMEM((D_MODEL, 2 * D_RNN), BF16),
        pltpu.VMEM((A_BANDS, A_WINDOW, 2 * A_BAND), BF16),
        pltpu.VMEM((D_RNN, D_MODEL), BF16),
        pltpu.VMEM((n_sub, lead + SUB_ROWS, D_RNN), F32),
        pltpu.VMEM((lead, D_RNN), F32),
        pltpu.VMEM((n_sub, SUB_ROWS, D_RNN), F32),
        pltpu.VMEM((n_sub, SUB_ROWS, D_RNN), F32),
        pltpu.VMEM((n_sub, SUB_ROWS, D_RNN), F32),
        pltpu.VMEM((SUBLANES, D_RNN), F32),
    ]
    return _mixer_call(_rglru_body, "rglru_mixer", TS_RGLRU, x, consts, scratch, mlp_w1, mlp_w2, i)


def _sgu_body(x_ref, g_ref, w_in_ref, ng_ref, w_s_ref, bias_ref, w_out_ref, o_ref,
              w_u_s, w_v_s, w_c_s, w_out_s, mix_ref):
    rows = x_ref.shape[1]

    @pl.when(_is_first_step())
    def _():
        w_u_s[...] = w_in_ref[:, :D_SGU].astype(BF16)
        w_v_s[...] = w_in_ref[:, D_SGU:].astype(BF16)
        w_out_s[...] = w_out_ref[...].astype(BF16)
        t_idx = lax.broadcasted_iota(jnp.int32, (SGU_CHUNK, SGU_CHUNK), 0)
        s_idx = lax.broadcasted_iota(jnp.int32, (SGU_CHUNK, SGU_CHUNK), 1)
        for grp in range(SGU_GROUPS):
            w_c_s[grp] = jnp.where(t_idx >= s_idx, w_s_ref[grp], 0.0).astype(BF16)

    def front(r0):
        h = _rmsnorm(x_ref[0, r0:r0 + SGU_SUB_ROWS, :], g_ref[...]).astype(BF16)
        v = _gelu_tanh(_dot(h, w_v_s[...]))
        v = _rmsnorm(v, ng_ref[...]).astype(BF16)
        return v, _gelu_tanh(_dot(h, w_u_s[...]))

    def back(r0, v, u):
        chunk_starts = range(0, SGU_SUB_ROWS, SGU_CHUNK)
        for grp in range(SGU_GROUPS):
            lanes = slice(grp * SGU_GROUP_DIM, (grp + 1) * SGU_GROUP_DIM)
            side_by_side = jnp.concatenate([v[c0:c0 + SGU_CHUNK, lanes] for c0 in chunk_starts], axis=1)
            mixed = _dot(w_c_s[grp], side_by_side)
            for n, c0 in enumerate(chunk_starts):
                mix_ref[r0 + c0:r0 + c0 + SGU_CHUNK, lanes] = (
                    mixed[:, n * SGU_GROUP_DIM:(n + 1) * SGU_GROUP_DIM] + bias_ref[:, lanes])
        y = (u * mix_ref[r0:r0 + SGU_SUB_ROWS, :]).astype(BF16)
        o_ref[0, r0:r0 + SGU_SUB_ROWS, :] = x_ref[0, r0:r0 + SGU_SUB_ROWS, :] + _dot(y, w_out_s[...])

    state = front(0)
    for r0 in range(0, rows, SGU_SUB_ROWS):
        state_next = front(r0 + SGU_SUB_ROWS) if r0 + SGU_SUB_ROWS < rows else None
        back(r0, *state)
        state = state_next


def _sgu_layer(x, i, j, g, w_in, norm_g, w_s, s_bias, w_out, mlp_w1, mlp_w2):
    bias = jnp.repeat(jnp.swapaxes(s_bias, 1, 2), SGU_GROUP_DIM, axis=2)
    consts = [(_row(g), i), (w_in, j), (_row(norm_g), j), (w_s, j), (bias, j), (w_out, j)]
    scratch = [
        pltpu.VMEM((D_MODEL, D_SGU), BF16),
        pltpu.VMEM((D_MODEL, D_SGU), BF16),
        pltpu.VMEM((SGU_GROUPS, SGU_CHUNK, SGU_CHUNK), BF16),
        pltpu.VMEM((D_SGU, D_MODEL), BF16),
        pltpu.VMEM((TS_SGU, D_SGU), F32),
    ]
    return _mixer_call(_sgu_body, "sgu_mixer", TS_SGU, x, consts, scratch, mlp_w1, mlp_w2, i)


def _shortconv_body(x_ref, g_ref, w_in_ref, conv_w_ref, w_out_ref, o_ref, w_in_s, w_out_s, hist_ref):
    rows = x_ref.shape[1]

    @pl.when(_is_first_step())
    def _():
        w_in_s[...] = w_in_ref[...].astype(BF16)
        w_out_s[...] = w_out_ref[...].astype(BF16)

    @pl.when(pl.program_id(1) == 0)
    def _():
        hist_ref[0:HIST, :] = jnp.zeros((HIST, D_CONV), F32)

    def front(r0):
        h = _rmsnorm(x_ref[0, r0:r0 + C_SUB_ROWS, :], g_ref[...]).astype(BF16)
        gb = _dot(h, w_in_s[:, :D_CONV])
        gc = _dot(h, w_in_s[:, D_CONV:2 * D_CONV])
        xv = _dot(h, w_in_s[:, 2 * D_CONV:])
        hist_ref[HIST + r0:HIST + r0 + C_SUB_ROWS, :] = gc * xv
        conv = None
        for k in range(C_CONV):
            start = HIST + r0 - (C_CONV - 1) + k
            term = conv_w_ref[k:k + 1, :] * hist_ref[start:start + C_SUB_ROWS, :]
            conv = term if conv is None else conv + term
        return (gb * conv).astype(BF16)

    y = front(0)
    for r0 in range(0, rows, C_SUB_ROWS):
        y_next = front(r0 + C_SUB_ROWS) if r0 + C_SUB_ROWS < rows else None
        o_ref[0, r0:r0 + C_SUB_ROWS, :] = x_ref[0, r0:r0 + C_SUB_ROWS, :] + _dot(y, w_out_s[...])
        y = y_next
    hist_ref[0:HIST, :] = hist_ref[rows:rows + HIST, :]


def _shortconv_layer(x, i, j, g, w_in, conv_w, w_out, mlp_w1, mlp_w2):
    consts = [(_row(g), i), (w_in, j), (conv_w, j), (w_out, j)]
    scratch = [
        pltpu.VMEM((D_MODEL, 3 * D_CONV), BF16),
        pltpu.VMEM((D_CONV, D_MODEL), BF16),
        pltpu.VMEM((HIST + TS_SHORTCONV, D_CONV), F32),
    ]
    return _mixer_call(_shortconv_body, "shortconv_mixer", TS_SHORTCONV, x, consts, scratch, mlp_w1, mlp_w2, i)


def _mlp_kernel(x_ref, g_ref, w1_ref, w2_ref, fg_ref, o_ref, *, final_norm):
    for r0 in range(0, x_ref.shape[0], MLP_SUB_ROWS):
        x = x_ref[r0:r0 + MLP_SUB_ROWS, :]
        h = _rmsnorm(x, g_ref[...]).astype(BF16)
        a = jnp.square(jnp.maximum(_dot(h, w1_ref[...]), 0.0)).astype(BF16)
        acc = x + _dot(a, w2_ref[...])
        if final_norm:
            acc = _rmsnorm(acc, fg_ref[...])
        o_ref[r0:r0 + MLP_SUB_ROWS, :] = acc


def _mlp_layer(x2d, i, g, w1_bf16, w2_bf16, final_g, final_norm):
    n, _ = x2d.shape
    tm = TM_MLP
    row_spec = pl.BlockSpec((tm, D_MODEL), lambda r: (r, 0))
    return pl.pallas_call(
        functools.partial(_mlp_kernel, final_norm=final_norm),
        grid=(n // tm,),
        in_specs=[
            row_spec,
            _layer_spec(_row(g), i),
            _layer_spec(w1_bf16[None], 0),
            _layer_spec(w2_bf16[None], 0),
            _layer_spec(final_g[None, None, :], 0),
        ],
        out_specs=row_spec,
        out_shape=jax.ShapeDtypeStruct(x2d.shape, F32),
        compiler_params=pltpu.CompilerParams(
            dimension_semantics=("arbitrary",), vmem_limit_bytes=VMEM_LIMIT),
        name="sqrelu_mlp",
    )(x2d, _row(g), w1_bf16[None], w2_bf16[None], final_g[None, None, :])


def kernel(x, norm_mix_g, norm_mlp_g, final_norm_g, a_w_in, a_conv_w, a_conv_b, a_gate_a_w, a_gate_a_b, a_gate_x_w, a_gate_x_b, a_lambda, a_w_out, b_w_in, b_norm_g, b_w_s, b_s_bias, b_w_out, c_w_in, c_conv_w, c_w_out, mlp_w1, mlp_w2):
    b, s, d = x.shape
    assert (b * s) % TM_MLP == 0 and TM_MLP % MLP_SUB_ROWS == 0 and SUB_ROWS % SGU_CHUNK == 0
    for i in range(DEPTH):
        kind, j = i % N_MIXERS, i // N_MIXERS
        if kind == 0:
            x, w1, w2 = _rglru_layer(x, i, j, norm_mix_g, a_w_in, a_conv_w, a_conv_b, a_gate_a_w,
                                     a_gate_a_b, a_gate_x_w, a_gate_x_b, a_lambda, a_w_out,
                                     mlp_w1, mlp_w2)
        elif kind == 1:
            x, w1, w2 = _sgu_layer(x, i, j, norm_mix_g, b_w_in, b_norm_g, b_w_s, b_s_bias, b_w_out,
                                   mlp_w1, mlp_w2)
        else:
            x, w1, w2 = _shortconv_layer(x, i, j, norm_mix_g, c_w_in, c_conv_w, c_w_out, mlp_w1, mlp_w2)
        x = _mlp_layer(x.reshape(b * s, d), i, norm_mlp_g, w1, w2, final_norm_g,
                       final_norm=(i == DEPTH - 1)).reshape(b, s, d)
    return x
```

```python
import functools

import jax
import jax.numpy as jnp
import numpy as np
from jax import lax
from jax.experimental import pallas as pl
from jax.experimental.pallas import tpu as pltpu

D_MODEL = 1024
DEPTH = 4
N_MIXERS = 3
D_RNN = 1280
A_HEADS = 16
A_HEAD_DIM = D_RNN // A_HEADS
A_CONV = 4
LRU_C = 8.0
D_SGU = D_MODEL
SGU_CHUNK = 128
SGU_GROUPS = 8
SGU_GROUP_DIM = D_SGU // SGU_GROUPS
D_CONV = D_MODEL
C_CONV = 3
D_FF = 4 * D_MODEL
EPS = 1e-6
LOG2_E = 1.4426950408889634

SUBLANES = 8
BF16_ROWS = 16
LANES = 128
MXU_COLS = 256
COL_TILE = 2 * MXU_COLS
A_BAND = MXU_COLS
A_BANDS = D_RNN // A_BAND
A_WINDOW = 2 * A_BAND
HIST = SUBLANES
TS_RGLRU = 512
TS_SGU = 1024
TS_SHORTCONV = 1024
SUB_ROWS = 256
C_SUB_ROWS = 512
SGU_SUB_ROWS = 512
TM_MLP = 1024
MLP_SUB_ROWS = 512
VMEM_LIMIT = 56 * 1024 * 1024

F32 = jnp.float32
BF16 = jnp.bfloat16


def _rmsnorm(x, g):
    return x * lax.rsqrt(jnp.mean(x * x, axis=-1, keepdims=True) + EPS) * g


def _dot(a, b):
    return jnp.dot(a, b, preferred_element_type=F32)


def _gelu_tanh(x):
    c = 0.7978845608028654
    half_x = 0.5 * x
    return half_x + half_x * jnp.tanh(x * (c + (c * 0.044715) * (x * x)))


def _layer_spec(stacked, layer):
    zeros = (0,) * (stacked.ndim - 1)
    return pl.BlockSpec((None,) + stacked.shape[1:], lambda *_: (layer,) + zeros,
                        pipeline_mode=pl.Buffered(1))


def _row(stacked):
    return stacked[:, None, :]


def _is_first_step():
    return jnp.logical_and(pl.program_id(0) == 0, pl.program_id(1) == 0)


def _mixer_call(body, name, ts, x, consts, scratch_shapes, mlp_w1, mlp_w2, mlp_layer):
    b, s, _ = x.shape
    assert s % ts == 0 and ts % SUB_ROWS == 0
    nt = s // ts
    steps = b * nt
    row_spec = pl.BlockSpec((1, ts, D_MODEL), lambda i, j: (i, j, 0))

    def slice_rows(w):
        rows = w.shape[1] // steps
        assert rows * steps == w.shape[1] and rows % BF16_ROWS == 0
        return rows

    def in_slice_spec(w):
        return pl.BlockSpec((None, slice_rows(w), w.shape[2]), lambda i, j: (mlp_layer, i * nt + j, 0))

    def out_slice_spec(w):
        return pl.BlockSpec((slice_rows(w), w.shape[2]), lambda i, j: (i * nt + j, 0))

    n_const = len(consts)

    def kernel(x_ref, *refs):
        const_refs = refs[:n_const]
        w1_ref, w2_ref, o_ref, w1_out_ref, w2_out_ref = refs[n_const:n_const + 5]
        w1_out_ref[...] = w1_ref[...].astype(BF16)
        w2_out_ref[...] = w2_ref[...].astype(BF16)
        body(x_ref, *const_refs, o_ref, *refs[n_const + 5:])

    return pl.pallas_call(
        kernel,
        grid=(b, nt),
        in_specs=[row_spec] + [_layer_spec(arr, layer) for arr, layer in consts]
        + [in_slice_spec(mlp_w1), in_slice_spec(mlp_w2)],
        out_specs=[row_spec, out_slice_spec(mlp_w1), out_slice_spec(mlp_w2)],
        out_shape=[jax.ShapeDtypeStruct(x.shape, F32),
                   jax.ShapeDtypeStruct(mlp_w1.shape[1:], BF16),
                   jax.ShapeDtypeStruct(mlp_w2.shape[1:], BF16)],
        scratch_shapes=scratch_shapes,
        compiler_params=pltpu.CompilerParams(
            dimension_semantics=("arbitrary", "arbitrary"), vmem_limit_bytes=VMEM_LIMIT),
        name=name,
    )(x, *[arr for arr, _ in consts], mlp_w1, mlp_w2)


def _sublane_scan(a, b):
    row = lax.broadcasted_iota(jnp.int32, a.shape, 0)
    for d in (1, 2, 4):
        keep = row >= d
        a_prev = jnp.where(keep, pltpu.roll(a, d, 0), 1.0)
        b_prev = jnp.where(keep, pltpu.roll(b, d, 0), 0.0)
        b = a * b_prev + b
        a = a * a_prev
    return a, b


def _segment_permutation(rows, inverse):
    seg = rows // SUBLANES
    i0 = lax.broadcasted_iota(jnp.int32, (rows, rows), 0)
    i1 = lax.broadcasted_iota(jnp.int32, (rows, rows), 1)
    major, natural = (i1, i0) if inverse else (i0, i1)
    hit = natural == (major % SUBLANES) * seg + major // SUBLANES
    return jnp.where(hit, 1.0, 0.0).astype(BF16)


def _band_window_starts():
    starts = []
    for band in range(A_BANDS):
        first_head = (band * A_BAND) // A_HEAD_DIM
        last_head = -(-((band + 1) * A_BAND) // A_HEAD_DIM)
        lo = min((first_head * A_HEAD_DIM) // LANES * LANES, D_RNN - A_WINDOW)
        assert lo <= first_head * A_HEAD_DIM and last_head * A_HEAD_DIM <= lo + A_WINDOW
        starts.append(lo)
    return starts


def _rglru_body(x_ref, g_ref, w_in_ref, conv_w_ref, conv_b_ref, gate_a_ref, gate_x_ref,
                head_tile_ref, head_mask_ref, b_a_ref, b_x_ref, lam_ref, w_out_ref, o_ref,
                w_in_s, w_band_s, w_out_s, ext_ref, tail_ref, xr_ref, a_ref, u_ref, carry_ref):
    window_starts = _band_window_starts()

    @pl.when(_is_first_step())
    def _():
        w_in_s[:, :D_RNN] = w_in_ref[:, D_RNN:].astype(BF16)
        w_in_s[:, D_RNN:] = w_in_ref[:, :D_RNN].astype(BF16)
        w_out_s[...] = w_out_ref[...].astype(BF16)
        for band, lo in enumerate(window_starts):
            for part, gate_ref in enumerate((gate_a_ref, gate_x_ref)):
                tiled = _dot(gate_ref[lo:lo + A_WINDOW, :].astype(BF16), head_tile_ref[band])
                w_band_s[band, :, part * A_BAND:(part + 1) * A_BAND] = (
                    tiled * head_mask_ref[band]).astype(BF16)

    @pl.when(pl.program_id(1) == 0)
    def _():
        tail_ref[...] = jnp.zeros(tail_ref.shape, F32)
        carry_ref[...] = jnp.zeros(carry_ref.shape, F32)

    n_sub = x_ref.shape[1] // SUB_ROWS

    def front(sub):
        x = x_ref[0, sub * SUB_ROWS:(sub + 1) * SUB_ROWS, :]
        return _rglru_front(x, g_ref, conv_w_ref, conv_b_ref, w_in_s,
                            ext_ref.at[sub], tail_ref, xr_ref.at[sub])

    state, _ = _interleave(front(0), None)
    for sub in range(n_sub):
        back = _rglru_back(*state, b_a_ref, b_x_ref, lam_ref, w_band_s, w_out_s,
                           xr_ref.at[sub], a_ref.at[sub], u_ref.at[sub], carry_ref)
        out, state = _interleave(back, front(sub + 1) if sub + 1 < n_sub else None)
        o_ref[0, sub * SUB_ROWS:(sub + 1) * SUB_ROWS, :] = out


def _interleave(first, second):
    results = [None, None]
    live = {k: g for k, g in enumerate((first, second)) if g is not None}
    while live:
        for k in list(live):
            try:
                next(live[k])
            except StopIteration as done:
                results[k] = done.value
                del live[k]
    return results


def _rglru_front(x, g_ref, conv_w_ref, conv_b_ref, w_in_s, ext_ref, tail_ref, xr_ref):
    rows = SUB_ROWS
    lead = (A_CONV - 1) * SUBLANES
    h = _rmsnorm(x, g_ref[...]).astype(BF16)
    hp = _dot(_segment_permutation(rows, False), h).astype(BF16)
    chunks = []
    for c0 in range(0, 2 * D_RNN, COL_TILE):
        chunks.append(_dot(hp, w_in_s[:, c0:c0 + COL_TILE]))
        yield
    proj = jnp.concatenate(chunks, axis=1)
    ext_ref[lead:lead + rows, :] = proj[:, :D_RNN]
    row8 = lax.broadcasted_iota(jnp.int32, (SUBLANES, D_RNN), 0)
    for j in range(A_CONV - 1):
        sl = slice(j * SUBLANES, (j + 1) * SUBLANES)
        cur = ext_ref[rows + j * SUBLANES:rows + (j + 1) * SUBLANES, :]
        ext_ref[sl, :] = jnp.where(row8 == 0, pltpu.roll(tail_ref[sl, :], 1, 0), pltpu.roll(cur, 1, 0))
        tail_ref[sl, :] = cur
    xh_all = 0.5 * conv_b_ref[...]
    for k in range(A_CONV):
        xh_all = xh_all + (0.5 * conv_w_ref[k:k + 1, :]) * ext_ref[k * SUBLANES:k * SUBLANES + rows, :]
    xr_ref[...] = xh_all
    return x, xh_all.astype(BF16), proj[:, D_RNN:]


def _rglru_back(x, xh_bf16, gate_pre, b_a_ref, b_x_ref, lam_ref, w_band_s, w_out_s,
                xr_ref, a_ref, u_ref, carry_ref):
    rows = SUB_ROWS
    seg = rows // SUBLANES
    window_starts = _band_window_starts()
    row8 = lax.broadcasted_iota(jnp.int32, (SUBLANES, A_BAND), 0)

    ys = []
    for band, lo in enumerate(window_starts):
        cols = slice(band * A_BAND, (band + 1) * A_BAND)
        xh = xr_ref[:, cols]
        half_pre = _dot(xh_bf16[:, lo:lo + A_WINDOW], w_band_s[band])
        t_r = jnp.tanh(half_pre[:, :A_BAND] + 0.5 * b_a_ref[:, cols])
        t_i = jnp.tanh(half_pre[:, A_BAND:] + 0.5 * b_x_ref[:, cols])
        lam = lam_ref[:, cols]
        softplus_neg_lam = jnp.maximum(-lam, 0.0) + jnp.log1p(jnp.exp(-jnp.abs(lam)))
        half_rate = (-0.5 * LRU_C * LOG2_E) * softplus_neg_lam
        a = jnp.exp2(half_rate * t_r + half_rate)
        a_ref[:, cols] = a
        z = 1.0 - a * a
        u_ref[:, cols] = jnp.where(z > 0.0, z * lax.rsqrt(z), 0.0) * (xh * (1.0 + t_i))

        def step(k):
            sl = slice(k * SUBLANES, (k + 1) * SUBLANES)
            return a_ref[sl, cols], u_ref[sl, cols], sl

        prod = jnp.ones((SUBLANES, A_BAND), F32)
        h_loc = jnp.zeros((SUBLANES, A_BAND), F32)
        for k in range(seg):
            a_k, u_k, _ = step(k)
            h_loc = a_k * h_loc + u_k
            prod = a_k * prod
        c_in = carry_ref[:, cols]
        cum_a, cum_b = _sublane_scan(prod, h_loc)
        seg_end = cum_a * c_in + cum_b
        carry_ref[:, cols] = jnp.broadcast_to(seg_end[SUBLANES - 1:SUBLANES, :], seg_end.shape)
        h_cur = jnp.where(row8 == 0, c_in, pltpu.roll(seg_end, 1, 0))
        for k in range(seg):
            a_k, u_k, sl = step(k)
            h_cur = a_k * h_cur + u_k
            u_ref[sl, cols] = h_cur

        gate = jax.nn.gelu(gate_pre[:, cols])
        ys.append((u_ref[:, cols] * gate).astype(BF16))
        yield

    y = _dot(_segment_permutation(rows, True), jnp.concatenate(ys, axis=1)).astype(BF16)
    return x + _dot(y, w_out_s[...])


def _rglru_layer(x, i, j, g, w_in, conv_w, conv_b, gate_a_w, gate_a_b, gate_x_w, gate_x_b, lam, w_out,
                 mlp_w1, mlp_w2):
    lead = (A_CONV - 1) * SUBLANES
    n_sub = TS_RGLRU // SUB_ROWS
    head_tile = np.zeros((A_BANDS, A_HEAD_DIM, A_BAND), np.float32)
    head_mask = np.zeros((A_BANDS, A_WINDOW, A_BAND), np.float32)
    for band, lo in enumerate(_band_window_starts()):
        col = band * A_BAND + np.arange(A_BAND)
        row = lo + np.arange(A_WINDOW)
        head_tile[band] = col[None, :] % A_HEAD_DIM == np.arange(A_HEAD_DIM)[:, None]
        head_mask[band] = row[:, None] // A_HEAD_DIM == col[None, :] // A_HEAD_DIM
    n_a = w_in.shape[0]
    consts = [
        (_row(g), i), (w_in, j), (conv_w, j), (_row(conv_b), j),
        (gate_a_w.reshape(n_a, D_RNN, A_HEAD_DIM), j), (gate_x_w.reshape(n_a, D_RNN, A_HEAD_DIM), j),
        (jnp.asarray(head_tile, BF16)[None], 0), (jnp.asarray(head_mask, F32)[None], 0),
        (_row(gate_a_b), j), (_row(gate_x_b), j), (_row(lam), j), (w_out, j),
    ]
    scratch = [
        pltpu.VMEM((D_MODEL, 2 * D_RNN), BF16),
        pltpu.VMEM((A_BANDS, A_WINDOW, 2 * A_BAND), BF16),
        pltpu.VMEM((D_RNN, D_MODEL), BF16),
        pltpu.VMEM((n_sub, lead + SUB_ROWS, D_RNN), F32),
        pltpu.VMEM((lead, D_RNN), F32),
        pltpu.VMEM((n_sub, SUB_ROWS, D_RNN), F32),
        pltpu.VMEM((n_sub, SUB_ROWS, D_RNN), F32),
        pltpu.VMEM((n_sub, SUB_ROWS, D_RNN), F32),
        pltpu.VMEM((SUBLANES, D_RNN), F32),
    ]
    return _mixer_call(_rglru_body, "rglru_mixer", TS_RGLRU, x, consts, scratch, mlp_w1, mlp_w2, i)


def _sgu_body(x_ref, g_ref, w_in_ref, ng_ref, w_s_ref, bias_ref, w_out_ref, o_ref,
              w_u_s, w_v_s, w_c_s, w_out_s, mix_ref):
    rows = x_ref.shape[1]

    @pl.when(_is_first_step())
    def _():
        w_u_s[...] = w_in_ref[:, :D_SGU].astype(BF16)
        w_v_s[...] = w_in_ref[:, D_SGU:].astype(BF16)
        w_out_s[...] = w_out_ref[...].astype(BF16)
        t_idx = lax.broadcasted_iota(jnp.int32, (SGU_CHUNK, SGU_CHUNK), 0)
        s_idx = lax.broadcasted_iota(jnp.int32, (SGU_CHUNK, SGU_CHUNK), 1)
        for grp in range(SGU_GROUPS):
            w_c_s[grp] = jnp.where(t_idx >= s_idx, w_s_ref[grp], 0.0).astype(BF16)

    def front(r0):
        h = _rmsnorm(x_ref[0, r0:r0 + SGU_SUB_ROWS, :], g_ref[...]).astype(BF16)
        v = _gelu_tanh(_dot(h, w_v_s[...]))
        v = _rmsnorm(v, ng_ref[...]).astype(BF16)
        return v, _gelu_tanh(_dot(h, w_u_s[...]))

    def back(r0, v, u):
        chunk_starts = range(0, SGU_SUB_ROWS, SGU_CHUNK)
        for grp in range(SGU_GROUPS):
            lanes = slice(grp * SGU_GROUP_DIM, (grp + 1) * SGU_GROUP_DIM)
            side_by_side = jnp.concatenate([v[c0:c0 + SGU_CHUNK, lanes] for c0 in chunk_starts], axis=1)
            mixed = _dot(w_c_s[grp], side_by_side)
            for n, c0 in enumerate(chunk_starts):
                mix_ref[r0 + c0:r0 + c0 + SGU_CHUNK, lanes] = (
                    mixed[:, n * SGU_GROUP_DIM:(n + 1) * SGU_GROUP_DIM] + bias_ref[:, lanes])
        y = (u * mix_ref[r0:r0 + SGU_SUB_ROWS, :]).astype(BF16)
        o_ref[0, r0:r0 + SGU_SUB_ROWS, :] = x_ref[0, r0:r0 + SGU_SUB_ROWS, :] + _dot(y, w_out_s[...])

    state = front(0)
    for r0 in range(0, rows, SGU_SUB_ROWS):
        state_next = front(r0 + SGU_SUB_ROWS) if r0 + SGU_SUB_ROWS < rows else None
        back(r0, *state)
        state = state_next


def _sgu_layer(x, i, j, g, w_in, norm_g, w_s, s_bias, w_out, mlp_w1, mlp_w2):
    bias = jnp.repeat(jnp.swapaxes(s_bias, 1, 2), SGU_GROUP_DIM, axis=2)
    consts = [(_row(g), i), (w_in, j), (_row(norm_g), j), (w_s, j), (bias, j), (w_out, j)]
    scratch = [
        pltpu.VMEM((D_MODEL, D_SGU), BF16),
        pltpu.VMEM((D_MODEL, D_SGU), BF16),
        pltpu.VMEM((SGU_GROUPS, SGU_CHUNK, SGU_CHUNK), BF16),
        pltpu.VMEM((D_SGU, D_MODEL), BF16),
        pltpu.VMEM((TS_SGU, D_SGU), F32),
    ]
    return _mixer_call(_sgu_body, "sgu_mixer", TS_SGU, x, consts, scratch, mlp_w1, mlp_w2, i)


def _shortconv_body(x_ref, g_ref, w_in_ref, conv_w_ref, w_out_ref, o_ref, w_in_s, w_out_s, hist_ref):
    rows = x_ref.shape[1]

    @pl.when(_is_first_step())
    def _():
        w_in_s[...] = w_in_ref[...].astype(BF16)
        w_out_s[...] = w_out_ref[...].astype(BF16)

    @pl.when(pl.program_id(1) == 0)
    def _():
        hist_ref[0:HIST, :] = jnp.zeros((HIST, D_CONV), F32)

    def front(r0):
        h = _rmsnorm(x_ref[0, r0:r0 + C_SUB_ROWS, :], g_ref[...]).astype(BF16)
        gb = _dot(h, w_in_s[:, :D_CONV])
        gc = _dot(h, w_in_s[:, D_CONV:2 * D_CONV])
        xv = _dot(h, w_in_s[:, 2 * D_CONV:])
        hist_ref[HIST + r0:HIST + r0 + C_SUB_ROWS, :] = gc * xv
        conv = None
        for k in range(C_CONV):
            start = HIST + r0 - (C_CONV - 1) + k
            term = conv_w_ref[k:k + 1, :] * hist_ref[start:start + C_SUB_ROWS, :]
            conv = term if conv is None else conv + term
        return (gb * conv).astype(BF16)

    y = front(0)
    for r0 in range(0, rows, C_SUB_ROWS):
        y_next = front(r0 + C_SUB_ROWS) if r0 + C_SUB_ROWS < rows else None
        o_ref[0, r0:r0 + C_SUB_ROWS, :] = x_ref[0, r0:r0 + C_SUB_ROWS, :] + _dot(y, w_out_s[...])
        y = y_next
    hist_ref[0:HIST, :] = hist_ref[rows:rows + HIST, :]


def _shortconv_layer(x, i, j, g, w_in, conv_w, w_out, mlp_w1, mlp_w2):
    consts = [(_row(g), i), (w_in, j), (conv_w, j), (w_out, j)]
    scratch = [
        pltpu.VMEM((D_MODEL, 3 * D_CONV), BF16),
        pltpu.VMEM((D_CONV, D_MODEL), BF16),
        pltpu.VMEM((HIST + TS_SHORTCONV, D_CONV), F32),
    ]
    return _mixer_call(_shortconv_body, "shortconv_mixer", TS_SHORTCONV, x, consts, scratch, mlp_w1, mlp_w2, i)


def _mlp_kernel(x_ref, g_ref, w1_ref, w2_ref, fg_ref, o_ref, *, final_norm):
    for r0 in range(0, x_ref.shape[0], MLP_SUB_ROWS):
        x = x_ref[r0:r0 + MLP_SUB_ROWS, :]
        h = _rmsnorm(x, g_ref[...]).astype(BF16)
        a = jnp.square(jnp.maximum(_dot(h, w1_ref[...]), 0.0)).astype(BF16)
        acc = x + _dot(a, w2_ref[...])
        if final_norm:
            acc = _rmsnorm(acc, fg_ref[...])
        o_ref[r0:r0 + MLP_SUB_ROWS, :] = acc


def _mlp_layer(x2d, i, g, w1_bf16, w2_bf16, final_g, final_norm):
    n, _ = x2d.shape
    tm = TM_MLP
    row_spec = pl.BlockSpec((tm, D_MODEL), lambda r: (r, 0))
    return pl.pallas_call(
        functools.partial(_mlp_kernel, final_norm=final_norm),
        grid=(n // tm,),
        in_specs=[
            row_spec,
            _layer_spec(_row(g), i),
            _layer_spec(w1_bf16[None], 0),
            _layer_spec(w2_bf16[None], 0),
            _layer_spec(final_g[None, None, :], 0),
        ],
        out_specs=row_spec,
        out_shape=jax.ShapeDtypeStruct(x2d.shape, F32),
        compiler_params=pltpu.CompilerParams(
            dimension_semantics=("arbitrary",), vmem_limit_bytes=VMEM_LIMIT),
        name="sqrelu_mlp",
    )(x2d, _row(g), w1_bf16[None], w2_bf16[None], final_g[None, None, :])


def kernel(x, norm_mix_g, norm_mlp_g, final_norm_g, a_w_in, a_conv_w, a_conv_b, a_gate_a_w, a_gate_a_b, a_gate_x_w, a_gate_x_b, a_lambda, a_w_out, b_w_in, b_norm_g, b_w_s, b_s_bias, b_w_out, c_w_in, c_conv_w, c_w_out, mlp_w1, mlp_w2):
    b, s, d = x.shape
    assert (b * s) % TM_MLP == 0 and TM_MLP % MLP_SUB_ROWS == 0 and SUB_ROWS % SGU_CHUNK == 0
    for i in range(DEPTH):
        kind, j = i % N_MIXERS, i // N_MIXERS
        if kind == 0:
            x, w1, w2 = _rglru_layer(x, i, j, norm_mix_g, a_w_in, a_conv_w, a_conv_b, a_gate_a_w,
                                     a_gate_a_b, a_gate_x_w, a_gate_x_b, a_lambda, a_w_out,
                                     mlp_w1, mlp_w2)
        elif kind == 1:
            x, w1, w2 = _sgu_layer(x, i, j, norm_mix_g, b_w_in, b_norm_g, b_w_s, b_s_bias, b_w_out,
                                   mlp_w1, mlp_w2)
        else:
            x, w1, w2 = _shortconv_layer(x, i, j, norm_mix_g, c_w_in, c_conv_w, c_w_out, mlp_w1, mlp_w2)
        x = _mlp_layer(x.reshape(b * s, d), i, norm_mlp_g, w1, w2, final_norm_g,
                       final_norm=(i == DEPTH - 1)).reshape(b, s, d)
    return x
```
